```python
import math
import jax, jax.numpy as jnp
from jax import lax
import numpy as np

D_MODEL = 1024
BATCH = 8
SEQ = 2048
DEPTH = 1
DEC_BATCH = 128
DEC_SEQ = 1
PAST_LEN = 16384
PAGE_SIZE = 128

ATTN_HEADS = 8
ATTN_KV_HEADS = 2
ATTN_HEAD_DIM = 64
WINDOW = 128
ATTN_BLOCK = 128
REL_BUCKETS = 32
REL_MAX_DIST = 128
RET_HEADS = 4
RET_DK = 128
RET_DV = 128
RET_CHUNK = 128
ROPE_BASE = 10000.0
D_FF = 2816
LN_EPS = 1e-5
GN_EPS = 1e-6
ALPHA = (2.0 * DEPTH) ** 0.25
BETA = (8.0 * DEPTH) ** -0.25

ATTN_Q = ATTN_HEADS * ATTN_HEAD_DIM
ATTN_KV = ATTN_KV_HEADS * ATTN_HEAD_DIM
RET_QK = RET_HEADS * RET_DK
RET_V = RET_HEADS * RET_DV
IN_COLS = ATTN_Q + 2 * ATTN_KV + 2 * RET_QK + 2 * RET_V + 2 * D_MODEL

kernel_name = "hybrid_swa_retention_macaron_step"


def layer_norm(x, g, b):
    xf = x.astype(jnp.float32)
    mu = xf.mean(-1, keepdims=True)
    var = jnp.square(xf - mu).mean(-1, keepdims=True)
    return ((xf - mu) * lax.rsqrt(var + LN_EPS) * g.astype(jnp.float32) + b.astype(jnp.float32)).astype(x.dtype)


def swiglu(x, w_up, w_down):
    gate, up = jnp.split(x @ w_up, 2, axis=-1)
    return (jax.nn.silu(gate) * up) @ w_down


def rel_bucket(dist):
    n = jnp.maximum(dist, 0)
    max_exact = REL_BUCKETS // 2
    ratio = jnp.maximum(n, 1).astype(jnp.float32) / max_exact
    large = max_exact + (jnp.log(jnp.maximum(ratio, 1.0)) / math.log(REL_MAX_DIST / max_exact)
                         * (REL_BUCKETS - max_exact)).astype(jnp.int32)
    large = jnp.minimum(large, REL_BUCKETS - 1)
    return jnp.where(n < max_exact, n, large)


def rotary(x, pos):
    half = x.shape[-1] // 2
    inv = ROPE_BASE ** (-jnp.arange(half, dtype=jnp.float32) / half)
    ang = pos.astype(jnp.float32)[:, None] * inv[None, :]
    cos = jnp.cos(ang)[None, :, None, :]
    sin = jnp.sin(ang)[None, :, None, :]
    xf = x.astype(jnp.float32)
    x1, x2 = xf[..., :half], xf[..., half:]
    return jnp.concatenate([x1 * cos - x2 * sin, x2 * cos + x1 * sin], axis=-1).astype(x.dtype)


def project(x, w_in, pos):
    B, L, _ = x.shape
    sizes = (ATTN_Q, ATTN_KV, ATTN_KV, RET_QK, RET_QK, RET_V, RET_V, D_MODEL, D_MODEL)
    idx = [int(i) for i in np.cumsum(sizes)[:-1]]
    qa, ka, va, qr, kr, vr, gr, ga, gb = jnp.split(x @ w_in, idx, axis=-1)
    qa = qa.reshape(B, L, ATTN_HEADS, ATTN_HEAD_DIM)
    ka = ka.reshape(B, L, ATTN_KV_HEADS, ATTN_HEAD_DIM)
    va = va.reshape(B, L, ATTN_KV_HEADS, ATTN_HEAD_DIM)
    qr = rotary(qr.reshape(B, L, RET_HEADS, RET_DK), pos)
    kr = rotary(kr.reshape(B, L, RET_HEADS, RET_DK), pos) * (RET_DK ** -0.5)
    vr = vr.reshape(B, L, RET_HEADS, RET_DV)
    return qa, ka, va, qr, kr, vr, gr, ga, gb


def window_attention(q, k, v, q_pos, k_pos, sinks, rel_bias):
    B, NB, Lq, H, hd = q.shape
    Lk = k.shape[2]
    G = H // ATTN_KV_HEADS
    qg = q.reshape(B, NB, Lq, ATTN_KV_HEADS, G, hd)
    s = jnp.einsum('bnqhgd,bnshd->bnhgqs', qg, k).astype(jnp.float32) * (hd ** -0.5)
    dist = q_pos[:, :, None] - k_pos[:, None, :]
    allowed = (dist >= 0) & (dist < WINDOW) & (k_pos[:, None, :] >= 0)
    bias = rel_bias.astype(jnp.float32)[rel_bucket(dist)]
    bias = bias.transpose(0, 3, 1, 2).reshape(NB, ATTN_KV_HEADS, G, Lq, Lk)
    s = jnp.where(allowed[None, :, None, None], s + bias[None], -jnp.inf)
    sink = sinks.astype(jnp.float32).reshape(ATTN_KV_HEADS, G)[None, None, :, :, None, None]
    m = jnp.maximum(s.max(-1, keepdims=True), sink)
    p = jnp.exp(s - m)
    p = p / (p.sum(-1, keepdims=True) + jnp.exp(sink - m))
    o = jnp.einsum('bnhgqs,bnshd->bnqhgd', p.astype(v.dtype), v)
    return o.reshape(B, NB, Lq, H * hd)


def retention_chunks(q, k, v, s0):
    C = q.shape[2]
    lg = jnp.log1p(-(2.0 ** (-5.0 - jnp.arange(RET_HEADS, dtype=jnp.float32))))
    i = jnp.arange(C, dtype=jnp.float32)
    diff = i[:, None] - i[None, :]
    dmask = jnp.where(diff[None] >= 0, jnp.exp(jnp.maximum(diff, 0.0)[None] * lg[:, None, None]), 0.0)
    scores = jnp.einsum('bnihd,bnjhd->bnhij', q, k) * dmask
    inner = jnp.einsum('bnhij,bnjhe->bnihe', scores, v)
    k_tail = k * jnp.exp((C - 1 - i)[:, None] * lg[None, :])[None, None, :, :, None]
    u = jnp.einsum('bnjhd,bnjhe->nbhde', k_tail, v)
    chunk_decay = jnp.exp(C * lg)[None, :, None, None]

    def step(s, u_n):
        return chunk_decay * s + u_n, s

    s_final, s_prev = lax.scan(step, s0, u)
    q_dec = q * jnp.exp((i + 1)[:, None] * lg[None, :])[None, None, :, :, None]
    cross = jnp.einsum('bnihd,nbhde->bnihe', q_dec, s_prev)
    return inner + cross, s_final


def head_group_norm(o):
    mu = o.mean(-1, keepdims=True)
    var = jnp.square(o - mu).mean(-1, keepdims=True)
    return (o - mu) * lax.rsqrt(var + GN_EPS)


def merge_branches(o_attn, o_ret, gr, ga, gb, w_attn_out, w_ret_out, w_o):
    B, L, _ = o_attn.shape
    r = head_group_norm(o_ret).reshape(B, L, RET_V).astype(gr.dtype) * jax.nn.silu(gr)
    merged = jax.nn.sigmoid(ga) * (o_attn @ w_attn_out) + jax.nn.sigmoid(gb) * (r @ w_ret_out)
    return merged @ w_o


def prompt_mixers(x, w_in, sinks, rel_bias, w_attn_out, w_ret_out, w_o, state_dtype):
    B, S, _ = x.shape
    pos = jnp.arange(S, dtype=jnp.int32)
    qa, ka, va, qr, kr, vr, gr, ga, gb = project(x, w_in, pos)
    nb = S // ATTN_BLOCK

    def band(t):
        tp = jnp.pad(t, ((0, 0), (ATTN_BLOCK, 0), (0, 0), (0, 0)))
        tp = tp.reshape(B, nb + 1, ATTN_BLOCK, ATTN_KV_HEADS, ATTN_HEAD_DIM)
        return jnp.concatenate([tp[:, :-1], tp[:, 1:]], axis=2)

    q_pos = pos.reshape(nb, ATTN_BLOCK)
    k_pos = jnp.concatenate([q_pos - ATTN_BLOCK, q_pos], axis=1)
    q_blk = qa.reshape(B, nb, ATTN_BLOCK, ATTN_HEADS, ATTN_HEAD_DIM)
    o_attn = window_attention(q_blk, band(ka), band(va), q_pos, k_pos, sinks, rel_bias).reshape(B, S, ATTN_Q)
    nc = S // RET_CHUNK

    def chunk(t):
        return t.astype(jnp.float32).reshape(B, nc, RET_CHUNK, RET_HEADS, t.shape[-1])

    s0 = jnp.zeros((B, RET_HEADS, RET_DK, RET_DV), jnp.float32)
    o_ret, s_fin = retention_chunks(chunk(qr), chunk(kr), chunk(vr), s0)
    o_ret = o_ret.reshape(B, S, RET_HEADS, RET_DV)
    y = merge_branches(o_attn, o_ret, gr, ga, gb, w_attn_out, w_ret_out, w_o)
    return y, ka[:, -WINDOW:], va[:, -WINDOW:], s_fin.astype(state_dtype)


def sample_mixers(x, k_buf, v_buf, s_prev, w_in, sinks, rel_bias, w_attn_out, w_ret_out, w_o):
    B, L, _ = x.shape
    pos = PAST_LEN + jnp.arange(L, dtype=jnp.int32)
    qa, ka, va, qr, kr, vr, gr, ga, gb = project(x, w_in, pos)
    w = k_buf.shape[1]
    k_all = jnp.concatenate([k_buf.astype(ka.dtype), ka], axis=1)
    v_all = jnp.concatenate([v_buf.astype(va.dtype), va], axis=1)
    k_pos = jnp.concatenate([PAST_LEN - w + jnp.arange(w, dtype=jnp.int32), pos])
    o_attn = window_attention(qa[:, None], k_all[:, None], v_all[:, None], pos[None], k_pos[None],
                              sinks, rel_bias).reshape(B, L, ATTN_Q)

    def one_chunk(t):
        return t.astype(jnp.float32)[:, None]

    o_ret, s_new = retention_chunks(one_chunk(qr), one_chunk(kr), one_chunk(vr), s_prev.astype(jnp.float32))
    o_ret = o_ret.reshape(B, L, RET_HEADS, RET_DV)
    y = merge_branches(o_attn, o_ret, gr, ga, gb, w_attn_out, w_ret_out, w_o)
    return y, k_all[:, -WINDOW:], v_all[:, -WINDOW:], s_new.astype(s_prev.dtype)


def setup_inputs(seed: int = 0) -> dict:
    key = jax.random.key(seed)
    ks = jax.random.split(key, 24)
    f32 = jnp.float32

    def nrm(k, shape, scale):
        return jax.random.normal(k, shape, f32) * scale

    return {
        "x_prompt": nrm(ks[0], (BATCH, SEQ, D_MODEL), 1.0),
        "x_sample": nrm(ks[1], (DEC_BATCH, DEC_SEQ, D_MODEL), 1.0),
        "cache_k_win": nrm(ks[2], (DEPTH, DEC_BATCH, WINDOW, ATTN_KV_HEADS, ATTN_HEAD_DIM), 1.0),
        "cache_v_win": nrm(ks[3], (DEPTH, DEC_BATCH, WINDOW, ATTN_KV_HEADS, ATTN_HEAD_DIM), 1.0),
        "state_ret": nrm(ks[4], (DEPTH, DEC_BATCH, RET_HEADS, RET_DK, RET_DV), 0.3),
        "rel_bias": nrm(ks[5], (REL_BUCKETS, ATTN_HEADS), 0.5),
        "w_in": nrm(ks[6], (DEPTH, D_MODEL, IN_COLS), D_MODEL ** -0.5),
        "attn_sinks": nrm(ks[7], (DEPTH, ATTN_HEADS), 0.5),
        "w_attn_out": nrm(ks[8], (DEPTH, ATTN_Q, D_MODEL), BETA * ATTN_Q ** -0.5),
        "w_ret_out": nrm(ks[9], (DEPTH, RET_V, D_MODEL), BETA * RET_V ** -0.5),
        "w_o": nrm(ks[10], (DEPTH, D_MODEL, D_MODEL), BETA * D_MODEL ** -0.5),
        "ffn1_w_up": nrm(ks[11], (DEPTH, D_MODEL, 2 * D_FF), D_MODEL ** -0.5),
        "ffn1_w_down": nrm(ks[12], (DEPTH, D_FF, D_MODEL), BETA * D_FF ** -0.5),
        "ffn2_w_up": nrm(ks[13], (DEPTH, D_MODEL, 2 * D_FF), D_MODEL ** -0.5),
        "ffn2_w_down": nrm(ks[14], (DEPTH, D_FF, D_MODEL), BETA * D_FF ** -0.5),
        "ln1_g": 1.0 + nrm(ks[15], (DEPTH, D_MODEL), 0.01),
        "ln1_b": nrm(ks[16], (DEPTH, D_MODEL), 0.01),
        "ln2_g": 1.0 + nrm(ks[17], (DEPTH, D_MODEL), 0.01),
        "ln2_b": nrm(ks[18], (DEPTH, D_MODEL), 0.01),
        "ln3_g": 1.0 + nrm(ks[19], (DEPTH, D_MODEL), 0.01),
        "ln3_b": nrm(ks[20], (DEPTH, D_MODEL), 0.01),
    }


def reference(x_prompt, x_sample, cache_k_win, cache_v_win, state_ret, rel_bias, w_in, attn_sinks,
              w_attn_out, w_ret_out, w_o, ffn1_w_up, ffn1_w_down, ffn2_w_up, ffn2_w_down,
              ln1_g, ln1_b, ln2_g, ln2_b, ln3_g, ln3_b):
    hp, hs = x_prompt, x_sample
    kp_l, vp_l, sp_l, ks_l, vs_l, ss_l = [], [], [], [], [], []
    for l in range(DEPTH):
        hp = layer_norm(ALPHA * hp + 0.5 * swiglu(hp, ffn1_w_up[l], ffn1_w_down[l]), ln1_g[l], ln1_b[l])
        hs = layer_norm(ALPHA * hs + 0.5 * swiglu(hs, ffn1_w_up[l], ffn1_w_down[l]), ln1_g[l], ln1_b[l])
        mp, kp, vp, sp = prompt_mixers(hp, w_in[l], attn_sinks[l], rel_bias, w_attn_out[l], w_ret_out[l], w_o[l],
                                       state_ret.dtype)
        ms, kbs, vbs, ss = sample_mixers(hs, cache_k_win[l], cache_v_win[l], state_ret[l], w_in[l], attn_sinks[l],
                                         rel_bias, w_attn_out[l], w_ret_out[l], w_o[l])
        hp = layer_norm(ALPHA * hp + mp, ln2_g[l], ln2_b[l])
        hs = layer_norm(ALPHA * hs + ms, ln2_g[l], ln2_b[l])
        hp = layer_norm(ALPHA * hp + 0.5 * swiglu(hp, ffn2_w_up[l], ffn2_w_down[l]), ln3_g[l], ln3_b[l])
        hs = layer_norm(ALPHA * hs + 0.5 * swiglu(hs, ffn2_w_up[l], ffn2_w_down[l]), ln3_g[l], ln3_b[l])
        kp_l.append(kp); vp_l.append(vp); sp_l.append(sp)
        ks_l.append(kbs); vs_l.append(vbs); ss_l.append(ss)
    k_win_prompt = jnp.stack(kp_l)
    v_win_prompt = jnp.stack(vp_l)
    state_ret_prompt = jnp.stack(sp_l)
    k_win_sample = jnp.stack(ks_l)
    v_win_sample = jnp.stack(vs_l)
    state_ret_sample = jnp.stack(ss_l)
    return (hp, hs, k_win_prompt, v_win_prompt, state_ret_prompt, k_win_sample, v_win_sample, state_ret_sample)
```

```python
import functools
import math

import numpy as np
import jax
import jax.numpy as jnp
from jax import lax
from jax.experimental import pallas as pl
from jax.experimental.pallas import tpu as pltpu

D_MODEL = 1024
DEPTH = 1
PAST_LEN = 16384
ATTN_HEADS = 8
ATTN_KV_HEADS = 2
ATTN_HEAD_DIM = 64
ATTN_GROUP = ATTN_HEADS // ATTN_KV_HEADS
WINDOW = 128
BLOCK = 128
REL_BUCKETS = 32
REL_MAX_DIST = 128
RET_HEADS = 4
RET_DK = 128
RET_DV = 128
ROPE_BASE = 10000.0
D_FF = 2816
LN_EPS = 1e-5
GN_EPS = 1e-6
ALPHA = (2.0 * DEPTH) ** 0.25

ATTN_Q = ATTN_HEADS * ATTN_HEAD_DIM
ATTN_KV = ATTN_KV_HEADS * ATTN_HEAD_DIM
RET_QK = RET_HEADS * RET_DK
RET_V = RET_HEADS * RET_DV
C_QA = 0
C_KA = C_QA + ATTN_Q
C_VA = C_KA + ATTN_KV
C_QR = C_VA + ATTN_KV
C_KR = C_QR + RET_QK
C_VR = C_KR + RET_QK
C_GR = C_VR + RET_V
C_GA = C_GR + RET_V
C_GB = C_GA + D_MODEL
IN_COLS = C_GB + D_MODEL
PROJ_GROUPS = ((C_QA, C_QR), (C_QR, C_KR), (C_KR, C_VR), (C_VR, C_GR), (C_GR, C_GA), (C_GA, C_GB), (C_GB, IN_COLS))

FF_CHUNK = 256
N_FF_CHUNKS = D_FF // FF_CHUNK
NEG = -1e30

F32 = jnp.float32
BF16 = jnp.bfloat16

VMEM_LIMIT_BYTES = 56 * 1024 * 1024


def _dot(a, b):
    return jnp.dot(a, b, preferred_element_type=F32)


def _dot_nt(a, b):
    return lax.dot_general(a, b, (((1,), (1,)), ((), ())), preferred_element_type=F32)


def _dot_tn(a, b):
    return lax.dot_general(a, b, (((0,), (0,)), ((), ())), preferred_element_type=F32)


def _layer_norm(z, g, b):
    mu = jnp.mean(z, axis=-1, keepdims=True)
    zc = z - mu
    var = jnp.mean(zc * zc, axis=-1, keepdims=True)
    return zc * lax.rsqrt(var + LN_EPS) * g + b


def _group_norm(o):
    mu = jnp.mean(o, axis=-1, keepdims=True)
    oc = o - mu
    var = jnp.mean(oc * oc, axis=-1, keepdims=True)
    return oc * lax.rsqrt(var + GN_EPS)


def _silu(x):
    return x * jax.nn.sigmoid(x)


def _const_spec(shape):
    nd = len(shape)
    return pl.BlockSpec(shape, lambda *_: (0,) * nd)


def _smem_spec():
    return pl.BlockSpec(memory_space=pltpu.SMEM)


def _ffn_ln_kernel(x_ref, wg_ref, wu_ref, wd_ref, g_ref, b_ref, o_ref, acc_ref):
    x = x_ref[...]
    xb = x.astype(BF16)
    for c in range(N_FF_CHUNKS):
        gate = _dot(xb, wg_ref[c])
        up = _dot(xb, wu_ref[c])
        a = (_silu(gate) * up).astype(BF16)
        d = _dot(a, wd_ref[c])
        if c == 0:
            acc_ref[...] = d
        else:
            acc_ref[...] += d
    z = ALPHA * x + 0.5 * acc_ref[...]
    o_ref[...] = _layer_norm(z, g_ref[...], b_ref[...])


def _ffn_ln(x2d, wg, wu, wd, g, b, tm):
    n = x2d.shape[0]
    assert n % tm == 0
    return pl.pallas_call(
        _ffn_ln_kernel,
        grid=(n // tm,),
        in_specs=[
            pl.BlockSpec((tm, D_MODEL), lambda i: (i, 0)),
            _const_spec(wg.shape),
            _const_spec(wu.shape),
            _const_spec(wd.shape),
            _const_spec(g.shape),
            _const_spec(b.shape),
        ],
        out_specs=pl.BlockSpec((tm, D_MODEL), lambda i: (i, 0)),
        out_shape=jax.ShapeDtypeStruct((n, D_MODEL), F32),
        scratch_shapes=[pltpu.VMEM((tm, D_MODEL), F32)],
        compiler_params=pltpu.CompilerParams(
            dimension_semantics=("arbitrary",), vmem_limit_bytes=VMEM_LIMIT_BYTES),
        name="ffn_ln",
    )(x2d, wg, wu, wd, g, b)


def _rotary(x, cos_full, sin_signed):
    return x * cos_full + pltpu.roll(x, RET_DK // 2, 1) * sin_signed


def _prompt_mixer_kernel(tile, h_ref, cos_ref, sin_ref, bucket_ref, relb_ref, sinks_ref, dmask_ref,
                         qdec_ref, tail_ref, cdec_ref, w_in_ref, w_ao_ref, w_ro_ref, w_o_ref, g_ref, b_ref,
                         h2_ref, kwin_ref, vwin_ref, sfin_ref,
                         proj_ref, bias_ref, kprev_ref, vprev_ref, s_ref, oattn_ref, r_ref):
    bi = pl.program_id(0)
    ti = pl.program_id(1)

    @pl.when(jnp.logical_and(bi == 0, ti == 0))
    def _build_bias():
        bucket = bucket_ref[...]
        for h in range(ATTN_HEADS):
            def body(k, acc, h=h):
                return jnp.where(bucket == k, relb_ref[k, h], acc)
            bias_ref[h] = lax.fori_loop(0, REL_BUCKETS, body, jnp.full((BLOCK, 2 * BLOCK), NEG, F32))

    @pl.when(ti == 0)
    def _reset_carry():
        kprev_ref[...] = jnp.zeros_like(kprev_ref)
        vprev_ref[...] = jnp.zeros_like(vprev_ref)
        s_ref[...] = jnp.zeros_like(s_ref)

    h_in = h_ref[...]
    hb = h_in.astype(BF16)
    for c0, c1 in PROJ_GROUPS:
        proj_ref[:, c0:c1] = _dot(hb, w_in_ref[:, c0:c1])

    col = lax.broadcasted_iota(jnp.int32, (BLOCK, 2 * BLOCK), 1)
    for sb in range(tile // BLOCK):
        rows = slice(sb * BLOCK, (sb + 1) * BLOCK)
        ka_cur = proj_ref[rows, C_KA:C_VA]
        va_cur = proj_ref[rows, C_VA:C_QR]
        k2 = jnp.concatenate([kprev_ref[...], ka_cur], axis=0)
        v2 = jnp.concatenate([vprev_ref[...], va_cur], axis=0)
        for kv in range(ATTN_KV_HEADS):
            kk = k2[:, kv * ATTN_HEAD_DIM:(kv + 1) * ATTN_HEAD_DIM].astype(BF16)
            vv = v2[:, kv * ATTN_HEAD_DIM:(kv + 1) * ATTN_HEAD_DIM].astype(BF16)
            for g in range(ATTN_GROUP):
                h = kv * ATTN_GROUP + g
                hc = slice(h * ATTN_HEAD_DIM, (h + 1) * ATTN_HEAD_DIM)
                q = proj_ref[rows, hc].astype(BF16)
                s = _dot_nt(q, kk) * (ATTN_HEAD_DIM ** -0.5) + bias_ref[h]
                if sb == 0:
                    s = jnp.where(jnp.logical_and(ti == 0, col < BLOCK), NEG, s)
                sink = sinks_ref[h]
                m = jnp.maximum(jnp.max(s, axis=-1, keepdims=True), sink)
                p = jnp.exp(s - m)
                denom = jnp.sum(p, axis=-1, keepdims=True) + jnp.exp(sink - m)
                p = p / denom
                oattn_ref[rows, hc] = _dot(p.astype(BF16), vv)
        kprev_ref[...] = ka_cur
        vprev_ref[...] = va_cur
        if sb == tile // BLOCK - 1:
            kwin_ref[...] = ka_cur
            vwin_ref[...] = va_cur

        cos_b = cos_ref[rows, :]
        sin_b = sin_ref[rows, :]
        for h in range(RET_HEADS):
            hc = slice(h * RET_DK, (h + 1) * RET_DK)
            qh = _rotary(proj_ref[rows, C_QR + h * RET_DK:C_QR + (h + 1) * RET_DK], cos_b, sin_b)
            kh = _rotary(proj_ref[rows, C_KR + h * RET_DK:C_KR + (h + 1) * RET_DK], cos_b, sin_b) * (RET_DK ** -0.5)
            vb = proj_ref[rows, C_VR + h * RET_DV:C_VR + (h + 1) * RET_DV].astype(BF16)
            qb = qh.astype(BF16)
            kb = kh.astype(BF16)
            scores = _dot_nt(qb, kb) * dmask_ref[h]
            inner = _dot(scores.astype(BF16), vb)
            s_prev = s_ref[h]
            cross = _dot((qh * qdec_ref[h]).astype(BF16), s_prev.astype(BF16))
            o = inner + cross
            u = _dot_tn((kh * tail_ref[h]).astype(BF16), vb)
            s_ref[h] = cdec_ref[h] * s_prev + u
            gr = proj_ref[rows, C_GR + h * RET_DV:C_GR + (h + 1) * RET_DV]
            r_ref[rows, hc] = _group_norm(o) * _silu(gr)

    @pl.when(ti == pl.num_programs(1) - 1)
    def _emit_state():
        sfin_ref[...] = s_ref[...]

    a_out = _dot(oattn_ref[...].astype(BF16), w_ao_ref[...])
    r_out = _dot(r_ref[...].astype(BF16), w_ro_ref[...])
    merged = (jax.nn.sigmoid(proj_ref[:, C_GA:C_GB]) * a_out
              + jax.nn.sigmoid(proj_ref[:, C_GB:IN_COLS]) * r_out)
    y = _dot(merged.astype(BF16), w_o_ref[...])
    h2_ref[...] = _layer_norm(ALPHA * h_in + y, g_ref[...], b_ref[...])


def _prompt_mixer(h, tabs, w_in, w_ao, w_ro, w_o, g, b, relb, sinks, tile):
    bsz, seq, _ = h.shape
    assert seq % tile == 0 and tile % BLOCK == 0
    nt = seq // tile
    kern = functools.partial(_prompt_mixer_kernel, tile)
    out_shape = (
        jax.ShapeDtypeStruct((bsz, seq, D_MODEL), F32),
        jax.ShapeDtypeStruct((bsz, WINDOW, ATTN_KV), F32),
        jax.ShapeDtypeStruct((bsz, WINDOW, ATTN_KV), F32),
        jax.ShapeDtypeStruct((bsz, RET_HEADS, RET_DK, RET_DV), F32),
    )
    return pl.pallas_call(
        kern,
        grid=(bsz, nt),
        in_specs=[
            pl.BlockSpec((None, tile, D_MODEL), lambda i, j: (i, j, 0)),
            pl.BlockSpec((tile, RET_DK), lambda i, j: (j, 0)),
            pl.BlockSpec((tile, RET_DK), lambda i, j: (j, 0)),
            _const_spec(tabs["bucket"].shape),
            _smem_spec(),
            _smem_spec(),
            _const_spec(tabs["dmask"].shape),
            _const_spec(tabs["qdec"].shape),
            _const_spec(tabs["tail"].shape),
            _smem_spec(),
            _const_spec(w_in.shape),
            _const_spec(w_ao.shape),
            _const_spec(w_ro.shape),
            _const_spec(w_o.shape),
            _const_spec(g.shape),
            _const_spec(b.shape),
        ],
        out_specs=(
            pl.BlockSpec((None, tile, D_MODEL), lambda i, j: (i, j, 0)),
            pl.BlockSpec((None, WINDOW, ATTN_KV), lambda i, j: (i, 0, 0)),
            pl.BlockSpec((None, WINDOW, ATTN_KV), lambda i, j: (i, 0, 0)),
            pl.BlockSpec((None, RET_HEADS, RET_DK, RET_DV), lambda i, j: (i, 0, 0, 0)),
        ),
        out_shape=out_shape,
        scratch_shapes=[
            pltpu.VMEM((tile, IN_COLS), F32),
            pltpu.VMEM((ATTN_HEADS, BLOCK, 2 * BLOCK), F32),
            pltpu.VMEM((BLOCK, ATTN_KV), F32),
            pltpu.VMEM((BLOCK, ATTN_KV), F32),
            pltpu.VMEM((RET_HEADS, RET_DK, RET_DV), F32),
            pltpu.VMEM((tile, ATTN_Q), F32),
            pltpu.VMEM((tile, RET_V), F32),
        ],
        compiler_params=pltpu.CompilerParams(
            dimension_semantics=("arbitrary", "arbitrary"), vmem_limit_bytes=VMEM_LIMIT_BYTES),
        name="prompt_mixer",
    )(h, tabs["cos"], tabs["sin"], tabs["bucket"], relb, sinks, tabs["dmask"], tabs["qdec"], tabs["tail"],
      tabs["cdec"], w_in, w_ao, w_ro, w_o, g, b)


def _sample_mixer_kernel(bt, hs_ref, cos_ref, sin_ref, sbucket_ref, relbt_ref, sinkv_ref, gam_ref,
                         w_in_ref, w_ao_ref, w_ro_ref, w_o_ref, g_ref, b_ref, ck_ref, cv_ref, st_ref,
                         ys_ref, kw_ref, vw_ref, stn_ref,
                         proj_ref, sbias_ref, qrot_ref, qdec_ref, kt_ref, cross_ref, oattn_ref):
    step = pl.program_id(0)
    nb = hs_ref.shape[0]

    @pl.when(step == 0)
    def _project():
        hb = hs_ref[...].astype(BF16)
        for c0, c1 in PROJ_GROUPS:
            proj_ref[:, c0:c1] = _dot(hb, w_in_ref[:, c0:c1])
        bucket = sbucket_ref[...]
        acc = jnp.full((ATTN_HEADS, WINDOW), NEG, F32)
        for k in range(REL_BUCKETS):
            acc = jnp.where(bucket == k, relbt_ref[:, k:k + 1], acc)
        sbias_ref[...] = acc
        cos_b = cos_ref[...]
        sin_b = sin_ref[...]
        for h in range(RET_HEADS):
            hc = slice(h * RET_DK, (h + 1) * RET_DK)
            qh = _rotary(proj_ref[:, C_QR + h * RET_DK:C_QR + (h + 1) * RET_DK], cos_b, sin_b)
            kh = _rotary(proj_ref[:, C_KR + h * RET_DK:C_KR + (h + 1) * RET_DK], cos_b, sin_b) * (RET_DK ** -0.5)
            qrot_ref[:, hc] = qh
            qdec_ref[:, hc] = qh * gam_ref[h]
            proj_ref[:, C_KR + h * RET_DK:C_KR + (h + 1) * RET_DK] = kh
            kt_ref[h] = kh.T

    r0 = pl.multiple_of(step * bt, bt)
    row8 = lax.broadcasted_iota(jnp.int32, (ATTN_HEADS, WINDOW), 0)
    lane8 = lax.broadcasted_iota(jnp.int32, (ATTN_HEADS, WINDOW), 1)
    half = ATTN_HEAD_DIM
    in_kv_half = (row8 < ATTN_GROUP) == (lane8 < half)
    need_roll = (row8 == 1) | (row8 == 3) | (row8 == 4) | (row8 == 6)
    row_w = lax.broadcasted_iota(jnp.int32, (WINDOW, ATTN_KV), 0)
    rowb = lax.broadcasted_iota(jnp.int32, (bt, RET_DV), 0)
    lane_b = lax.broadcasted_iota(jnp.int32, (RET_DK, nb), 1)
    sinkv = sinkv_ref[...]
    bias_new = relbt_ref[:, 0:1]
    sbias = sbias_ref[...]
    cross_acc = [jnp.zeros((bt, RET_DV), F32) for _ in range(RET_HEADS)]
    oattn_acc = [jnp.zeros((bt, 128), F32) for _ in range(ATTN_Q // 128)]
    qa_tile = [proj_ref[pl.ds(r0, bt), C_QA + k * 128:C_QA + (k + 1) * 128] for k in range(ATTN_Q // 128)]
    ka_tile = proj_ref[pl.ds(r0, bt), C_KA:C_VA]
    va_tile = proj_ref[pl.ds(r0, bt), C_VA:C_QR]

    for bl in range(bt):
        bg = r0 + bl
        qrow = [jnp.broadcast_to(qa_tile[k][bl:bl + 1, :], (ATTN_HEADS, 128)) for k in range(4)]
        gsel = jnp.where(row8 < 2, qrow[0], jnp.where(row8 < 4, qrow[1], jnp.where(row8 < 6, qrow[2], qrow[3])))
        gsel = jnp.where(need_roll, pltpu.roll(gsel, half, 1), gsel)
        qb8 = jnp.where(in_kv_half, gsel, 0.0).astype(BF16)
        kc = ck_ref[bl]
        vc = cv_ref[bl]
        knew = ka_tile[bl:bl + 1, :]
        vnew = va_tile[bl:bl + 1, :]
        scale = ATTN_HEAD_DIM ** -0.5
        s = _dot_nt(qb8, kc.astype(BF16)) * scale + sbias
        s_new = jnp.sum(qb8.astype(F32) * knew.astype(BF16).astype(F32), axis=-1, keepdims=True) * scale + bias_new
        m = jnp.maximum(jnp.maximum(jnp.max(s, axis=-1, keepdims=True), s_new), sinkv)
        p = jnp.exp(s - m)
        p_new = jnp.exp(s_new - m)
        denom = jnp.sum(p, axis=-1, keepdims=True) + p_new + jnp.exp(sinkv - m)
        p = p / denom
        p_new = p_new / denom
        o8 = _dot(p.astype(BF16), vc.astype(BF16)) + p_new.astype(BF16).astype(F32) * vnew.astype(BF16).astype(F32)
        o8 = jnp.where(need_roll, pltpu.roll(o8, half, 1), o8)
        o8 = jnp.where(((row8 % 2) == 0) == (lane8 < half), o8, 0.0)
        for k in range(4):
            pair = o8[2 * k:2 * k + 1, :] + o8[2 * k + 1:2 * k + 2, :]
            oattn_acc[k] = jnp.where(rowb == bl, pair, oattn_acc[k])
        kw_ref[bl] = jnp.where(row_w == WINDOW - 1, knew, pltpu.roll(kc, WINDOW - 1, 0))
        vw_ref[bl] = jnp.where(row_w == WINDOW - 1, vnew, pltpu.roll(vc, WINDOW - 1, 0))
        for h in range(RET_HEADS):
            hc = slice(h * RET_DK, (h + 1) * RET_DK)
            s_prev = st_ref[bl, h]
            qd = qdec_ref[pl.ds(r0, bt), hc].astype(BF16)
            res = _dot(qd, s_prev.astype(BF16))
            cross_acc[h] = jnp.where(rowb == bl, res, cross_acc[h])
            k_col = jnp.where(lane_b == bg, kt_ref[h], 0.0).astype(BF16)
            v_all = proj_ref[:, C_VR + h * RET_DV:C_VR + (h + 1) * RET_DV].astype(BF16)
            u = _dot(k_col, v_all)
            stn_ref[bl, h] = gam_ref[h] * s_prev + u
    for h in range(RET_HEADS):
        cross_ref[pl.ds(r0, bt), h * RET_DV:(h + 1) * RET_DV] = cross_acc[h]
    for k in range(ATTN_Q // 128):
        oattn_ref[pl.ds(r0, bt), k * 128:(k + 1) * 128] = oattn_acc[k]

    @pl.when(step == pl.num_programs(0) - 1)
    def _merge():
        for h in range(RET_HEADS):
            hc = slice(h * RET_DK, (h + 1) * RET_DK)
            qf = qrot_ref[:, hc].astype(BF16).astype(F32)
            kf = proj_ref[:, C_KR + h * RET_DK:C_KR + (h + 1) * RET_DK].astype(BF16).astype(F32)
            vf = proj_ref[:, C_VR + h * RET_DV:C_VR + (h + 1) * RET_DV].astype(BF16).astype(F32)
            score = jnp.sum(qf * kf, axis=-1, keepdims=True)
            o = score.astype(BF16).astype(F32) * vf + cross_ref[:, hc]
            gr = proj_ref[:, C_GR + h * RET_DV:C_GR + (h + 1) * RET_DV]
            cross_ref[:, hc] = _group_norm(o) * _silu(gr)
        a_out = _dot(oattn_ref[...].astype(BF16), w_ao_ref[...])
        r_out = _dot(cross_ref[...].astype(BF16), w_ro_ref[...])
        merged = (jax.nn.sigmoid(proj_ref[:, C_GA:C_GB]) * a_out
                  + jax.nn.sigmoid(proj_ref[:, C_GB:IN_COLS]) * r_out)
        y = _dot(merged.astype(BF16), w_o_ref[...])
        ys_ref[...] = _layer_norm(ALPHA * hs_ref[...] + y, g_ref[...], b_ref[...])


def _sample_mixer(hs, ck, cv, st, tabs, w_in, w_ao, w_ro, w_o, g, b, relbt, sinkv, bt):
    nb = hs.shape[0]
    assert nb % bt == 0
    kern = functools.partial(_sample_mixer_kernel, bt)
    out_shape = (
        jax.ShapeDtypeStruct((nb, D_MODEL), F32),
        jax.ShapeDtypeStruct((nb, WINDOW, ATTN_KV), F32),
        jax.ShapeDtypeStruct((nb, WINDOW, ATTN_KV), F32),
        jax.ShapeDtypeStruct((nb, RET_HEADS, RET_DK, RET_DV), F32),
    )
    return pl.pallas_call(
        kern,
        grid=(nb // bt,),
        in_specs=[
            _const_spec(hs.shape),
            _const_spec(tabs["cos_s"].shape),
            _const_spec(tabs["sin_s"].shape),
            _const_spec(tabs["sbucket"].shape),
            _const_spec(relbt.shape),
            _const_spec(sinkv.shape),
            _smem_spec(),
            _const_spec(w_in.shape),
            _const_spec(w_ao.shape),
            _const_spec(w_ro.shape),
            _const_spec(w_o.shape),
            _const_spec(g.shape),
            _const_spec(b.shape),
            pl.BlockSpec((bt, WINDOW, ATTN_KV), lambda i: (i, 0, 0)),
            pl.BlockSpec((bt, WINDOW, ATTN_KV), lambda i: (i, 0, 0)),
            pl.BlockSpec((bt, RET_HEADS, RET_DK, RET_DV), lambda i: (i, 0, 0, 0)),
        ],
        out_specs=(
            _const_spec((nb, D_MODEL)),
            pl.BlockSpec((bt, WINDOW, ATTN_KV), lambda i: (i, 0, 0)),
            pl.BlockSpec((bt, WINDOW, ATTN_KV), lambda i: (i, 0, 0)),
            pl.BlockSpec((bt, RET_HEADS, RET_DK, RET_DV), lambda i: (i, 0, 0, 0)),
        ),
        out_shape=out_shape,
        scratch_shapes=[
            pltpu.VMEM((nb, IN_COLS), F32),
            pltpu.VMEM((ATTN_HEADS, WINDOW), F32),
            pltpu.VMEM((nb, RET_QK), F32),
            pltpu.VMEM((nb, RET_QK), F32),
            pltpu.VMEM((RET_HEADS, RET_DK, nb), F32),
            pltpu.VMEM((nb, RET_V), F32),
            pltpu.VMEM((nb, ATTN_Q), F32),
        ],
        compiler_params=pltpu.CompilerParams(
            dimension_semantics=("arbitrary",), vmem_limit_bytes=VMEM_LIMIT_BYTES),
        name="sample_mixer",
    )(hs, tabs["cos_s"], tabs["sin_s"], tabs["sbucket"], relbt, sinkv, tabs["gamma"],
      w_in, w_ao, w_ro, w_o, g, b, ck, cv, st)


def _rel_bucket_np(dist):
    n = np.maximum(dist, 0)
    max_exact = REL_BUCKETS // 2
    ratio = np.maximum(n, 1).astype(np.float32) / np.float32(max_exact)
    large = max_exact + (np.log(np.maximum(ratio, np.float32(1.0))) / np.float32(math.log(REL_MAX_DIST / max_exact))
                         * np.float32(REL_BUCKETS - max_exact)).astype(np.int32)
    large = np.minimum(large, REL_BUCKETS - 1)
    return np.where(n < max_exact, n, large).astype(np.int32)


def _tables(seq):
    half = RET_DK // 2
    inv = ROPE_BASE ** (-jnp.arange(half, dtype=F32) / half)

    def rope(pos):
        ang = pos.astype(F32)[:, None] * inv[None, :]
        cos, sin = jnp.cos(ang), jnp.sin(ang)
        return jnp.concatenate([cos, cos], axis=-1), jnp.concatenate([-sin, sin], axis=-1)

    cos_p, sin_p = rope(jnp.arange(seq, dtype=jnp.int32))
    cos_s, sin_s = rope(PAST_LEN + jnp.arange(1, dtype=jnp.int32))

    qi = np.arange(BLOCK)[:, None]
    kj = np.arange(2 * BLOCK)[None, :]
    dist = BLOCK + qi - kj
    allowed = (dist >= 0) & (dist < WINDOW)
    bucket = np.where(allowed, _rel_bucket_np(dist), -1).astype(np.int32)
    sdist = WINDOW - np.arange(WINDOW)
    sbucket = np.where(sdist < WINDOW, _rel_bucket_np(sdist), -1).astype(np.int32)
    sbucket = np.broadcast_to(sbucket[None, :], (ATTN_HEADS, WINDOW))

    lg = jnp.log1p(-(2.0 ** (-5.0 - jnp.arange(RET_HEADS, dtype=F32))))
    i = jnp.arange(BLOCK, dtype=F32)
    diff = i[:, None] - i[None, :]
    dmask = jnp.where(diff[None] >= 0, jnp.exp(jnp.maximum(diff, 0.0)[None] * lg[:, None, None]), 0.0)
    ones = jnp.ones((RET_HEADS, BLOCK, RET_DK), F32)
    qdec = jnp.exp((i + 1)[None, :] * lg[:, None])[:, :, None] * ones
    tail = jnp.exp((BLOCK - 1 - i)[None, :] * lg[:, None])[:, :, None] * ones
    return {
        "cos": cos_p, "sin": sin_p, "cos_s": cos_s, "sin_s": sin_s,
        "bucket": jnp.asarray(bucket), "sbucket": jnp.asarray(sbucket),
        "dmask": dmask, "qdec": qdec, "tail": tail,
        "cdec": jnp.exp(BLOCK * lg), "gamma": jnp.exp(lg),
    }


def _ffn_weights(w_up, w_down):
    def chunked(w):
        return w.astype(BF16).reshape(D_MODEL, N_FF_CHUNKS, FF_CHUNK).transpose(1, 0, 2)
    return chunked(w_up[:, :D_FF]), chunked(w_up[:, D_FF:]), w_down.astype(BF16).reshape(N_FF_CHUNKS, FF_CHUNK, D_MODEL)


FFN_TILE = 512
MIX_TILE = 256
SAMPLE_BT = 8


def kernel(x_prompt, x_sample, cache_k_win, cache_v_win, state_ret, rel_bias, w_in, attn_sinks, w_attn_out,
           w_ret_out, w_o, ffn1_w_up, ffn1_w_down, ffn2_w_up, ffn2_w_down, ln1_g, ln1_b, ln2_g, ln2_b, ln3_g, ln3_b):
    bsz, seq, _ = x_prompt.shape
    nb = x_sample.shape[0]
    tabs = _tables(seq)
    hp = x_prompt.reshape(bsz * seq, D_MODEL)
    hs = x_sample.reshape(nb, D_MODEL)
    outs = [[] for _ in range(6)]
    row = lambda v: v.reshape(1, D_MODEL)
    for l in range(DEPTH):
        f1 = _ffn_weights(ffn1_w_up[l], ffn1_w_down[l])
        f2 = _ffn_weights(ffn2_w_up[l], ffn2_w_down[l])
        w_in_b = w_in[l].astype(BF16)
        w_ao_b = w_attn_out[l].astype(BF16)
        w_ro_b = w_ret_out[l].astype(BF16)
        w_o_b = w_o[l].astype(BF16)
        sinks = attn_sinks[l]

        hp = _ffn_ln(hp, *f1, row(ln1_g[l]), row(ln1_b[l]), FFN_TILE)
        hs = _ffn_ln(hs, *f1, row(ln1_g[l]), row(ln1_b[l]), nb)

        hp3, kp, vp, sp = _prompt_mixer(hp.reshape(bsz, seq, D_MODEL), tabs, w_in_b, w_ao_b, w_ro_b, w_o_b,
                                        row(ln2_g[l]), row(ln2_b[l]), rel_bias, sinks, MIX_TILE)
        hs, ks, vs, ss = _sample_mixer(
            hs, cache_k_win[l].reshape(nb, WINDOW, ATTN_KV), cache_v_win[l].reshape(nb, WINDOW, ATTN_KV),
            state_ret[l], tabs, w_in_b, w_ao_b, w_ro_b, w_o_b, row(ln2_g[l]), row(ln2_b[l]),
            rel_bias.T, sinks.reshape(ATTN_HEADS, 1), SAMPLE_BT)
        hp = hp3.reshape(bsz * seq, D_MODEL)

        hp = _ffn_ln(hp, *f2, row(ln3_g[l]), row(ln3_b[l]), FFN_TILE)
        hs = _ffn_ln(hs, *f2, row(ln3_g[l]), row(ln3_b[l]), nb)

        kv_shape = (WINDOW, ATTN_KV_HEADS, ATTN_HEAD_DIM)
        for acc, v in zip(outs, (kp.reshape(bsz, *kv_shape), vp.reshape(bsz, *kv_shape), sp,
                                 ks.reshape(nb, *kv_shape), vs.reshape(nb, *kv_shape), ss)):
            acc.append(v)
    stacked = [jnp.stack(v) for v in outs]
    return (hp.reshape(bsz, seq, D_MODEL), hs.reshape(nb, 1, D_MODEL), *stacked)
```

```python
import functools
import math

import numpy as np
import jax
import jax.numpy as jnp
from jax import lax
from jax.experimental import pallas as pl
from jax.experimental.pallas import tpu as pltpu

D_MODEL = 1024
DEPTH = 1
PAST_LEN = 16384
ATTN_HEADS = 8
ATTN_KV_HEADS = 2
ATTN_HEAD_DIM = 64
ATTN_GROUP = ATTN_HEADS // ATTN_KV_HEADS
WINDOW = 128
BLOCK = 128
REL_BUCKETS = 32
REL_MAX_DIST = 128
RET_HEADS = 4
RET_DK = 128
RET_DV = 128
ROPE_BASE = 10000.0
D_FF = 2816
LN_EPS = 1e-5
GN_EPS = 1e-6
ALPHA = (2.0 * DEPTH) ** 0.25

ATTN_Q = ATTN_HEADS * ATTN_HEAD_DIM
ATTN_KV = ATTN_KV_HEADS * ATTN_HEAD_DIM
RET_QK = RET_HEADS * RET_DK
RET_V = RET_HEADS * RET_DV
C_QA = 0
C_KA = C_QA + ATTN_Q
C_VA = C_KA + ATTN_KV
C_QR = C_VA + ATTN_KV
C_KR = C_QR + RET_QK
C_VR = C_KR + RET_QK
C_GR = C_VR + RET_V
C_GA = C_GR + RET_V
C_GB = C_GA + D_MODEL
IN_COLS = C_GB + D_MODEL
QKV_GROUPS = ((C_QA, C_QR), (C_QR, C_KR), (C_KR, C_VR), (C_VR, C_GR))
GATE_GROUPS = ((C_GR, C_GA), (C_GA, C_GB), (C_GB, IN_COLS))
PROJ_GROUPS = QKV_GROUPS + GATE_GROUPS

FF_CHUNK = 256
N_FF_CHUNKS = D_FF // FF_CHUNK
NEG = -1e30

F32 = jnp.float32
BF16 = jnp.bfloat16

VMEM_LIMIT_BYTES = 56 * 1024 * 1024


def _dot(a, b):
    return jnp.dot(a, b, preferred_element_type=F32)


def _dot_nt(a, b):
    return lax.dot_general(a, b, (((1,), (1,)), ((), ())), preferred_element_type=F32)


def _dot_tn(a, b):
    return lax.dot_general(a, b, (((0,), (0,)), ((), ())), preferred_element_type=F32)


def _layer_norm(z, g, b):
    mu = jnp.mean(z, axis=-1, keepdims=True)
    zc = z - mu
    var = jnp.mean(zc * zc, axis=-1, keepdims=True)
    return zc * lax.rsqrt(var + LN_EPS) * g + b


def _group_norm(o):
    mu = jnp.mean(o, axis=-1, keepdims=True)
    oc = o - mu
    var = jnp.mean(oc * oc, axis=-1, keepdims=True)
    return oc * lax.rsqrt(var + GN_EPS)


def _silu(x):
    return x * jax.nn.sigmoid(x)


def _const_spec(shape):
    nd = len(shape)
    return pl.BlockSpec(shape, lambda *_: (0,) * nd)


def _smem_spec():
    return pl.BlockSpec(memory_space=pltpu.SMEM)


def _ffn_ln_kernel(x_ref, wg_ref, wu_ref, wd_ref, g_ref, b_ref, o_ref, acc_ref):
    x = x_ref[...]
    xb = x.astype(BF16)
    for c in range(N_FF_CHUNKS):
        gate = _dot(xb, wg_ref[c])
        up = _dot(xb, wu_ref[c])
        a = (_silu(gate) * up).astype(BF16)
        d = _dot(a, wd_ref[c])
        if c == 0:
            acc_ref[...] = d
        else:
            acc_ref[...] += d
    z = ALPHA * x + 0.5 * acc_ref[...]
    o_ref[...] = _layer_norm(z, g_ref[...], b_ref[...])


def _ffn_ln(x2d, wg, wu, wd, g, b, tm):
    n = x2d.shape[0]
    assert n % tm == 0
    return pl.pallas_call(
        _ffn_ln_kernel,
        grid=(n // tm,),
        in_specs=[
            pl.BlockSpec((tm, D_MODEL), lambda i: (i, 0)),
            _const_spec(wg.shape),
            _const_spec(wu.shape),
            _const_spec(wd.shape),
            _const_spec(g.shape),
            _const_spec(b.shape),
        ],
        out_specs=pl.BlockSpec((tm, D_MODEL), lambda i: (i, 0)),
        out_shape=jax.ShapeDtypeStruct((n, D_MODEL), F32),
        scratch_shapes=[pltpu.VMEM((tm, D_MODEL), F32)],
        compiler_params=pltpu.CompilerParams(
            dimension_semantics=("arbitrary",), vmem_limit_bytes=VMEM_LIMIT_BYTES),
        name="ffn_ln",
    )(x2d, wg, wu, wd, g, b)


def _rotary(x, cos_full, sin_signed):
    return x * cos_full + pltpu.roll(x, RET_DK // 2, 1) * sin_signed


def _prompt_mixer_kernel(tile, h_ref, cos_ref, sin_ref, bucket_ref, relb_ref, sinks_ref, dmask_ref,
                         qdec_ref, tail_ref, cdec_ref, w_in_ref, w_ao_ref, w_ro_ref, w_o_ref, g_ref, b_ref,
                         h2_ref, kwin_ref, vwin_ref, sfin_ref,
                         proj_ref, bias_ref, kprev_ref, vprev_ref, s_ref, oattn_ref, r_ref):
    bi = pl.program_id(0)
    ti = pl.program_id(1)

    @pl.when(jnp.logical_and(bi == 0, ti == 0))
    def _build_bias():
        bucket = bucket_ref[...]
        for h in range(ATTN_HEADS):
            def body(k, acc, h=h):
                return jnp.where(bucket == k, relb_ref[k, h], acc)
            bias_ref[h] = lax.fori_loop(0, REL_BUCKETS, body, jnp.full((BLOCK, 2 * BLOCK), NEG, F32))

    @pl.when(ti == 0)
    def _reset_carry():
        kprev_ref[...] = jnp.zeros_like(kprev_ref)
        vprev_ref[...] = jnp.zeros_like(vprev_ref)
        s_ref[...] = jnp.zeros_like(s_ref)

    nsb = tile // BLOCK
    col = lax.broadcasted_iota(jnp.int32, (BLOCK, 2 * BLOCK), 1)

    def rows_of(sb):
        return slice(sb * BLOCK, (sb + 1) * BLOCK)

    def project(sb, groups):
        rows = rows_of(sb)
        hb = h_ref[rows, :].astype(BF16)
        for c0, c1 in groups:
            proj_ref[rows, c0:c1] = _dot(hb, w_in_ref[:, c0:c1])

    def mix_first(sb):
        rows = rows_of(sb)
        ka_cur = proj_ref[rows, C_KA:C_VA]
        va_cur = proj_ref[rows, C_VA:C_QR]
        if sb == 0:
            ka_prev, va_prev = kprev_ref[...], vprev_ref[...]
        else:
            ka_prev = proj_ref[rows_of(sb - 1), C_KA:C_VA]
            va_prev = proj_ref[rows_of(sb - 1), C_VA:C_QR]
        k2 = jnp.concatenate([ka_prev, ka_cur], axis=0)
        v2 = jnp.concatenate([va_prev, va_cur], axis=0)
        probs, vals = [], []
        for kv in range(ATTN_KV_HEADS):
            kk = k2[:, kv * ATTN_HEAD_DIM:(kv + 1) * ATTN_HEAD_DIM].astype(BF16)
            vals.append(v2[:, kv * ATTN_HEAD_DIM:(kv + 1) * ATTN_HEAD_DIM].astype(BF16))
            for g in range(ATTN_GROUP):
                h = kv * ATTN_GROUP + g
                q = proj_ref[rows, h * ATTN_HEAD_DIM:(h + 1) * ATTN_HEAD_DIM].astype(BF16)
                s = _dot_nt(q, kk) * (ATTN_HEAD_DIM ** -0.5) + bias_ref[h]
                if sb == 0:
                    s = jnp.where(jnp.logical_and(ti == 0, col < BLOCK), NEG, s)
                sink = sinks_ref[h]
                m = jnp.maximum(jnp.max(s, axis=-1, keepdims=True), sink)
                p = jnp.exp(s - m)
                denom = jnp.sum(p, axis=-1, keepdims=True) + jnp.exp(sink - m)
                probs.append((p / denom).astype(BF16))
        if sb == nsb - 1:
            kprev_ref[...] = ka_cur
            vprev_ref[...] = va_cur
            kwin_ref[...] = ka_cur
            vwin_ref[...] = va_cur

        cos_b = cos_ref[rows, :]
        sin_b = sin_ref[rows, :]
        scores, vbs, crosses = [], [], []
        for h in range(RET_HEADS):
            qh = _rotary(proj_ref[rows, C_QR + h * RET_DK:C_QR + (h + 1) * RET_DK], cos_b, sin_b)
            kh = _rotary(proj_ref[rows, C_KR + h * RET_DK:C_KR + (h + 1) * RET_DK], cos_b, sin_b) * (RET_DK ** -0.5)
            vb = proj_ref[rows, C_VR + h * RET_DV:C_VR + (h + 1) * RET_DV].astype(BF16)
            scores.append((_dot_nt(qh.astype(BF16), kh.astype(BF16)) * dmask_ref[h]).astype(BF16))
            s_prev = s_ref[h]
            crosses.append(_dot((qh * qdec_ref[h]).astype(BF16), s_prev.astype(BF16)))
            u = _dot_tn((kh * tail_ref[h]).astype(BF16), vb)
            s_ref[h] = cdec_ref[h] * s_prev + u
            vbs.append(vb)
        return probs, vals, scores, vbs, crosses

    def mix_second(sb, operands):
        rows = rows_of(sb)
        probs, vals, scores, vbs, crosses = operands
        for h in range(ATTN_HEADS):
            oattn_ref[rows, h * ATTN_HEAD_DIM:(h + 1) * ATTN_HEAD_DIM] = _dot(probs[h], vals[h // ATTN_GROUP])
        for h in range(RET_HEADS):
            o = _dot(scores[h], vbs[h]) + crosses[h]
            gr = proj_ref[rows, C_GR + h * RET_DV:C_GR + (h + 1) * RET_DV]
            r_ref[rows, h * RET_DV:(h + 1) * RET_DV] = _group_norm(o) * _silu(gr)

    def merge(sb):
        rows = rows_of(sb)
        a_out = _dot(oattn_ref[rows, :].astype(BF16), w_ao_ref[...])
        r_out = _dot(r_ref[rows, :].astype(BF16), w_ro_ref[...])
        merged = (jax.nn.sigmoid(proj_ref[rows, C_GA:C_GB]) * a_out
                  + jax.nn.sigmoid(proj_ref[rows, C_GB:IN_COLS]) * r_out)
        y = _dot(merged.astype(BF16), w_o_ref[...])
        h2_ref[rows, :] = _layer_norm(ALPHA * h_ref[rows, :] + y, g_ref[...], b_ref[...])

    project(0, QKV_GROUPS)
    for sb in range(nsb):
        operands = mix_first(sb)
        project(sb, GATE_GROUPS)
        mix_second(sb, operands)
        if sb + 1 < nsb:
            project(sb + 1, QKV_GROUPS)
        merge(sb)

    @pl.when(ti == pl.num_programs(1) - 1)
    def _emit_state():
        sfin_ref[...] = s_ref[...]


def _prompt_mixer(h, tabs, w_in, w_ao, w_ro, w_o, g, b, relb, sinks, tile):
    bsz, seq, _ = h.shape
    assert seq % tile == 0 and tile % BLOCK == 0
    nt = seq // tile
    kern = functools.partial(_prompt_mixer_kernel, tile)
    out_shape = (
        jax.ShapeDtypeStruct((bsz, seq, D_MODEL), F32),
        jax.ShapeDtypeStruct((bsz, WINDOW, ATTN_KV), F32),
        jax.ShapeDtypeStruct((bsz, WINDOW, ATTN_KV), F32),
        jax.ShapeDtypeStruct((bsz, RET_HEADS, RET_DK, RET_DV), F32),
    )
    return pl.pallas_call(
        kern,
        grid=(bsz, nt),
        in_specs=[
            pl.BlockSpec((None, tile, D_MODEL), lambda i, j: (i, j, 0)),
            pl.BlockSpec((tile, RET_DK), lambda i, j: (j, 0)),
            pl.BlockSpec((tile, RET_DK), lambda i, j: (j, 0)),
            _const_spec(tabs["bucket"].shape),
            _smem_spec(),
            _smem_spec(),
            _const_spec(tabs["dmask"].shape),
            _const_spec(tabs["qdec"].shape),
            _const_spec(tabs["tail"].shape),
            _smem_spec(),
            _const_spec(w_in.shape),
            _const_spec(w_ao.shape),
            _const_spec(w_ro.shape),
            _const_spec(w_o.shape),
            _const_spec(g.shape),
            _const_spec(b.shape),
        ],
        out_specs=(
            pl.BlockSpec((None, tile, D_MODEL), lambda i, j: (i, j, 0)),
            pl.BlockSpec((None, WINDOW, ATTN_KV), lambda i, j: (i, 0, 0)),
            pl.BlockSpec((None, WINDOW, ATTN_KV), lambda i, j: (i, 0, 0)),
            pl.BlockSpec((None, RET_HEADS, RET_DK, RET_DV), lambda i, j: (i, 0, 0, 0)),
        ),
        out_shape=out_shape,
        scratch_shapes=[
            pltpu.VMEM((tile, IN_COLS), F32),
            pltpu.VMEM((ATTN_HEADS, BLOCK, 2 * BLOCK), F32),
            pltpu.VMEM((BLOCK, ATTN_KV), F32),
            pltpu.VMEM((BLOCK, ATTN_KV), F32),
            pltpu.VMEM((RET_HEADS, RET_DK, RET_DV), F32),
            pltpu.VMEM((tile, ATTN_Q), F32),
            pltpu.VMEM((tile, RET_V), F32),
        ],
        compiler_params=pltpu.CompilerParams(
            dimension_semantics=("arbitrary", "arbitrary"), vmem_limit_bytes=VMEM_LIMIT_BYTES),
        name="prompt_mixer",
    )(h, tabs["cos"], tabs["sin"], tabs["bucket"], relb, sinks, tabs["dmask"], tabs["qdec"], tabs["tail"],
      tabs["cdec"], w_in, w_ao, w_ro, w_o, g, b)


def _sample_mixer_kernel(bt, hs_ref, cos_ref, sin_ref, sbucket_ref, relbt_ref, sinkv_ref, gam_ref,
                         w_in_ref, w_ao_ref, w_ro_ref, w_o_ref, g_ref, b_ref, ck_ref, cv_ref, st_ref,
                         ys_ref, kw_ref, vw_ref, stn_ref,
                         proj_ref, sbias_ref, qrot_ref, qdec_ref, kt_ref, cross_ref, oattn_ref):
    step = pl.program_id(0)
    nb = hs_ref.shape[0]

    @pl.when(step == 0)
    def _project():
        hb = hs_ref[...].astype(BF16)
        for c0, c1 in PROJ_GROUPS:
            proj_ref[:, c0:c1] = _dot(hb, w_in_ref[:, c0:c1])
        bucket = sbucket_ref[...]
        acc = jnp.full((ATTN_HEADS, WINDOW), NEG, F32)
        for k in range(REL_BUCKETS):
            acc = jnp.where(bucket == k, relbt_ref[:, k:k + 1], acc)
        sbias_ref[...] = acc
        cos_b = cos_ref[...]
        sin_b = sin_ref[...]
        for h in range(RET_HEADS):
            hc = slice(h * RET_DK, (h + 1) * RET_DK)
            qh = _rotary(proj_ref[:, C_QR + h * RET_DK:C_QR + (h + 1) * RET_DK], cos_b, sin_b)
            kh = _rotary(proj_ref[:, C_KR + h * RET_DK:C_KR + (h + 1) * RET_DK], cos_b, sin_b) * (RET_DK ** -0.5)
            qrot_ref[:, hc] = qh
            qdec_ref[:, hc] = qh * gam_ref[h]
            proj_ref[:, C_KR + h * RET_DK:C_KR + (h + 1) * RET_DK] = kh
            kt_ref[h] = kh.T

    r0 = pl.multiple_of(step * bt, bt)
    row8 = lax.broadcasted_iota(jnp.int32, (ATTN_HEADS, WINDOW), 0)
    lane8 = lax.broadcasted_iota(jnp.int32, (ATTN_HEADS, WINDOW), 1)
    half = ATTN_HEAD_DIM
    in_kv_half = (row8 < ATTN_GROUP) == (lane8 < half)
    need_roll = (row8 == 1) | (row8 == 3) | (row8 == 4) | (row8 == 6)
    row_w = lax.broadcasted_iota(jnp.int32, (WINDOW, ATTN_KV), 0)
    rowb = lax.broadcasted_iota(jnp.int32, (bt, RET_DV), 0)
    lane_b = lax.broadcasted_iota(jnp.int32, (RET_DK, nb), 1)
    sinkv = sinkv_ref[...]
    bias_new = relbt_ref[:, 0:1]
    sbias = sbias_ref[...]
    cross_acc = [jnp.zeros((bt, RET_DV), F32) for _ in range(RET_HEADS)]
    oattn_acc = [jnp.zeros((bt, 128), F32) for _ in range(ATTN_Q // 128)]
    qa_tile = [proj_ref[pl.ds(r0, bt), C_QA + k * 128:C_QA + (k + 1) * 128] for k in range(ATTN_Q // 128)]
    ka_tile = proj_ref[pl.ds(r0, bt), C_KA:C_VA]
    va_tile = proj_ref[pl.ds(r0, bt), C_VA:C_QR]

    for bl in range(bt):
        bg = r0 + bl
        qrow = [jnp.broadcast_to(qa_tile[k][bl:bl + 1, :], (ATTN_HEADS, 128)) for k in range(4)]
        gsel = jnp.where(row8 < 2, qrow[0], jnp.where(row8 < 4, qrow[1], jnp.where(row8 < 6, qrow[2], qrow[3])))
        gsel = jnp.where(need_roll, pltpu.roll(gsel, half, 1), gsel)
        qb8 = jnp.where(in_kv_half, gsel, 0.0).astype(BF16)
        kc = ck_ref[bl]
        vc = cv_ref[bl]
        knew = ka_tile[bl:bl + 1, :]
        vnew = va_tile[bl:bl + 1, :]
        scale = ATTN_HEAD_DIM ** -0.5
        s = _dot_nt(qb8, kc.astype(BF16)) * scale + sbias
        s_new = jnp.sum(qb8.astype(F32) * knew.astype(BF16).astype(F32), axis=-1, keepdims=True) * scale + bias_new
        m = jnp.maximum(jnp.maximum(jnp.max(s, axis=-1, keepdims=True), s_new), sinkv)
        p = jnp.exp(s - m)
        p_new = jnp.exp(s_new - m)
        denom = jnp.sum(p, axis=-1, keepdims=True) + p_new + jnp.exp(sinkv - m)
        p = p / denom
        p_new = p_new / denom
        o8 = _dot(p.astype(BF16), vc.astype(BF16)) + p_new.astype(BF16).astype(F32) * vnew.astype(BF16).astype(F32)
        o8 = jnp.where(need_roll, pltpu.roll(o8, half, 1), o8)
        o8 = jnp.where(((row8 % 2) == 0) == (lane8 < half), o8, 0.0)
        for k in range(4):
            pair = o8[2 * k:2 * k + 1, :] + o8[2 * k + 1:2 * k + 2, :]
            oattn_acc[k] = jnp.where(rowb == bl, pair, oattn_acc[k])
        kw_ref[bl] = jnp.where(row_w == WINDOW - 1, knew, pltpu.roll(kc, WINDOW - 1, 0))
        vw_ref[bl] = jnp.where(row_w == WINDOW - 1, vnew, pltpu.roll(vc, WINDOW - 1, 0))
        for h in range(RET_HEADS):
            hc = slice(h * RET_DK, (h + 1) * RET_DK)
            s_prev = st_ref[bl, h]
            qd = qdec_ref[pl.ds(r0, bt), hc].astype(BF16)
            res = _dot(qd, s_prev.astype(BF16))
            cross_acc[h] = jnp.where(rowb == bl, res, cross_acc[h])
            k_col = jnp.where(lane_b == bg, kt_ref[h], 0.0).astype(BF16)
            v_all = proj_ref[:, C_VR + h * RET_DV:C_VR + (h + 1) * RET_DV].astype(BF16)
            u = _dot(k_col, v_all)
            stn_ref[bl, h] = gam_ref[h] * s_prev + u
    for h in range(RET_HEADS):
        cross_ref[pl.ds(r0, bt), h * RET_DV:(h + 1) * RET_DV] = cross_acc[h]
    for k in range(ATTN_Q // 128):
        oattn_ref[pl.ds(r0, bt), k * 128:(k + 1) * 128] = oattn_acc[k]

    @pl.when(step == pl.num_programs(0) - 1)
    def _merge():
        for h in range(RET_HEADS):
            hc = slice(h * RET_DK, (h + 1) * RET_DK)
            qf = qrot_ref[:, hc].astype(BF16).astype(F32)
            kf = proj_ref[:, C_KR + h * RET_DK:C_KR + (h + 1) * RET_DK].astype(BF16).astype(F32)
            vf = proj_ref[:, C_VR + h * RET_DV:C_VR + (h + 1) * RET_DV].astype(BF16).astype(F32)
            score = jnp.sum(qf * kf, axis=-1, keepdims=True)
            o = score.astype(BF16).astype(F32) * vf + cross_ref[:, hc]
            gr = proj_ref[:, C_GR + h * RET_DV:C_GR + (h + 1) * RET_DV]
            cross_ref[:, hc] = _group_norm(o) * _silu(gr)
        a_out = _dot(oattn_ref[...].astype(BF16), w_ao_ref[...])
        r_out = _dot(cross_ref[...].astype(BF16), w_ro_ref[...])
        merged = (jax.nn.sigmoid(proj_ref[:, C_GA:C_GB]) * a_out
                  + jax.nn.sigmoid(proj_ref[:, C_GB:IN_COLS]) * r_out)
        y = _dot(merged.astype(BF16), w_o_ref[...])
        ys_ref[...] = _layer_norm(ALPHA * hs_ref[...] + y, g_ref[...], b_ref[...])


def _sample_mixer(hs, ck, cv, st, tabs, w_in, w_ao, w_ro, w_o, g, b, relbt, sinkv, bt):
    nb = hs.shape[0]
    assert nb % bt == 0
    kern = functools.partial(_sample_mixer_kernel, bt)
    out_shape = (
        jax.ShapeDtypeStruct((nb, D_MODEL), F32),
        jax.ShapeDtypeStruct((nb, WINDOW, ATTN_KV), F32),
        jax.ShapeDtypeStruct((nb, WINDOW, ATTN_KV), F32),
        jax.ShapeDtypeStruct((nb, RET_HEADS, RET_DK, RET_DV), F32),
    )
    return pl.pallas_call(
        kern,
        grid=(nb // bt,),
        in_specs=[
            _const_spec(hs.shape),
            _const_spec(tabs["cos_s"].shape),
            _const_spec(tabs["sin_s"].shape),
            _const_spec(tabs["sbucket"].shape),
            _const_spec(relbt.shape),
            _const_spec(sinkv.shape),
            _smem_spec(),
            _const_spec(w_in.shape),
            _const_spec(w_ao.shape),
            _const_spec(w_ro.shape),
            _const_spec(w_o.shape),
            _const_spec(g.shape),
            _const_spec(b.shape),
            pl.BlockSpec((bt, WINDOW, ATTN_KV), lambda i: (i, 0, 0)),
            pl.BlockSpec((bt, WINDOW, ATTN_KV), lambda i: (i, 0, 0)),
            pl.BlockSpec((bt, RET_HEADS, RET_DK, RET_DV), lambda i: (i, 0, 0, 0)),
        ],
        out_specs=(
            _const_spec((nb, D_MODEL)),
            pl.BlockSpec((bt, WINDOW, ATTN_KV), lambda i: (i, 0, 0)),
            pl.BlockSpec((bt, WINDOW, ATTN_KV), lambda i: (i, 0, 0)),
            pl.BlockSpec((bt, RET_HEADS, RET_DK, RET_DV), lambda i: (i, 0, 0, 0)),
        ),
        out_shape=out_shape,
        scratch_shapes=[
            pltpu.VMEM((nb, IN_COLS), F32),
            pltpu.VMEM((ATTN_HEADS, WINDOW), F32),
            pltpu.VMEM((nb, RET_QK), F32),
            pltpu.VMEM((nb, RET_QK), F32),
            pltpu.VMEM((RET_HEADS, RET_DK, nb), F32),
            pltpu.VMEM((nb, RET_V), F32),
            pltpu.VMEM((nb, ATTN_Q), F32),
        ],
        compiler_params=pltpu.CompilerParams(
            dimension_semantics=("arbitrary",), vmem_limit_bytes=VMEM_LIMIT_BYTES),
        name="sample_mixer",
    )(hs, tabs["cos_s"], tabs["sin_s"], tabs["sbucket"], relbt, sinkv, tabs["gamma"],
      w_in, w_ao, w_ro, w_o, g, b, ck, cv, st)


def _rel_bucket_np(dist):
    n = np.maximum(dist, 0)
    max_exact = REL_BUCKETS // 2
    ratio = np.maximum(n, 1).astype(np.float32) / np.float32(max_exact)
    large = max_exact + (np.log(np.maximum(ratio, np.float32(1.0))) / np.float32(math.log(REL_MAX_DIST / max_exact))
                         * np.float32(REL_BUCKETS - max_exact)).astype(np.int32)
    large = np.minimum(large, REL_BUCKETS - 1)
    return np.where(n < max_exact, n, large).astype(np.int32)


def _tables(seq):
    half = RET_DK // 2
    inv = ROPE_BASE ** (-jnp.arange(half, dtype=F32) / half)

    def rope(pos):
        ang = pos.astype(F32)[:, None] * inv[None, :]
        cos, sin = jnp.cos(ang), jnp.sin(ang)
        return jnp.concatenate([cos, cos], axis=-1), jnp.concatenate([-sin, sin], axis=-1)

    cos_p, sin_p = rope(jnp.arange(seq, dtype=jnp.int32))
    cos_s, sin_s = rope(PAST_LEN + jnp.arange(1, dtype=jnp.int32))

    qi = np.arange(BLOCK)[:, None]
    kj = np.arange(2 * BLOCK)[None, :]
    dist = BLOCK + qi - kj
    allowed = (dist >= 0) & (dist < WINDOW)
    bucket = np.where(allowed, _rel_bucket_np(dist), -1).astype(np.int32)
    sdist = WINDOW - np.arange(WINDOW)
    sbucket = np.where(sdist < WINDOW, _rel_bucket_np(sdist), -1).astype(np.int32)
    sbucket = np.broadcast_to(sbucket[None, :], (ATTN_HEADS, WINDOW))

    lg = jnp.log1p(-(2.0 ** (-5.0 - jnp.arange(RET_HEADS, dtype=F32))))
    i = jnp.arange(BLOCK, dtype=F32)
    diff = i[:, None] - i[None, :]
    dmask = jnp.where(diff[None] >= 0, jnp.exp(jnp.maximum(diff, 0.0)[None] * lg[:, None, None]), 0.0)
    ones = jnp.ones((RET_HEADS, BLOCK, RET_DK), F32)
    qdec = jnp.exp((i + 1)[None, :] * lg[:, None])[:, :, None] * ones
    tail = jnp.exp((BLOCK - 1 - i)[None, :] * lg[:, None])[:, :, None] * ones
    return {
        "cos": cos_p, "sin": sin_p, "cos_s": cos_s, "sin_s": sin_s,
        "bucket": jnp.asarray(bucket), "sbucket": jnp.asarray(sbucket),
        "dmask": dmask, "qdec": qdec, "tail": tail,
        "cdec": jnp.exp(BLOCK * lg), "gamma": jnp.exp(lg),
    }


def _ffn_weights(w_up, w_down):
    def chunked(w):
        return w.astype(BF16).reshape(D_MODEL, N_FF_CHUNKS, FF_CHUNK).transpose(1, 0, 2)
    return chunked(w_up[:, :D_FF]), chunked(w_up[:, D_FF:]), w_down.astype(BF16).reshape(N_FF_CHUNKS, FF_CHUNK, D_MODEL)


FFN_TILE = 512
MIX_TILE = 512
SAMPLE_BT = 8


def kernel(x_prompt, x_sample, cache_k_win, cache_v_win, state_ret, rel_bias, w_in, attn_sinks, w_attn_out,
           w_ret_out, w_o, ffn1_w_up, ffn1_w_down, ffn2_w_up, ffn2_w_down, ln1_g, ln1_b, ln2_g, ln2_b, ln3_g, ln3_b):
    bsz, seq, _ = x_prompt.shape
    nb = x_sample.shape[0]
    tabs = _tables(seq)
    hp = x_prompt.reshape(bsz * seq, D_MODEL)
    hs = x_sample.reshape(nb, D_MODEL)
    outs = [[] for _ in range(6)]
    row = lambda v: v.reshape(1, D_MODEL)
    for l in range(DEPTH):
        f1 = _ffn_weights(ffn1_w_up[l], ffn1_w_down[l])
        f2 = _ffn_weights(ffn2_w_up[l], ffn2_w_down[l])
        w_in_b = w_in[l].astype(BF16)
        w_ao_b = w_attn_out[l].astype(BF16)
        w_ro_b = w_ret_out[l].astype(BF16)
        w_o_b = w_o[l].astype(BF16)
        sinks = attn_sinks[l]

        hp = _ffn_ln(hp, *f1, row(ln1_g[l]), row(ln1_b[l]), FFN_TILE)
        hs = _ffn_ln(hs, *f1, row(ln1_g[l]), row(ln1_b[l]), nb)

        hp3, kp, vp, sp = _prompt_mixer(hp.reshape(bsz, seq, D_MODEL), tabs, w_in_b, w_ao_b, w_ro_b, w_o_b,
                                        row(ln2_g[l]), row(ln2_b[l]), rel_bias, sinks, MIX_TILE)
        hs, ks, vs, ss = _sample_mixer(
            hs, cache_k_win[l].reshape(nb, WINDOW, ATTN_KV), cache_v_win[l].reshape(nb, WINDOW, ATTN_KV),
            state_ret[l], tabs, w_in_b, w_ao_b, w_ro_b, w_o_b, row(ln2_g[l]), row(ln2_b[l]),
            rel_bias.T, sinks.reshape(ATTN_HEADS, 1), SAMPLE_BT)
        hp = hp3.reshape(bsz * seq, D_MODEL)

        hp = _ffn_ln(hp, *f2, row(ln3_g[l]), row(ln3_b[l]), FFN_TILE)
        hs = _ffn_ln(hs, *f2, row(ln3_g[l]), row(ln3_b[l]), nb)

        kv_shape = (WINDOW, ATTN_KV_HEADS, ATTN_HEAD_DIM)
        for acc, v in zip(outs, (kp.reshape(bsz, *kv_shape), vp.reshape(bsz, *kv_shape), sp,
                                 ks.reshape(nb, *kv_shape), vs.reshape(nb, *kv_shape), ss)):
            acc.append(v)
    stacked = [jnp.stack(v) for v in outs]
    return (hp.reshape(bsz, seq, D_MODEL), hs.reshape(nb, 1, D_MODEL), *stacked)
```

```python
import functools
import math

import numpy as np
import jax
import jax.numpy as jnp
from jax import lax
from jax.experimental import pallas as pl
from jax.experimental.pallas import tpu as pltpu

D_MODEL = 1024
DEPTH = 1
PAST_LEN = 16384
ATTN_HEADS = 8
ATTN_KV_HEADS = 2
ATTN_HEAD_DIM = 64
ATTN_GROUP = ATTN_HEADS // ATTN_KV_HEADS
WINDOW = 128
BLOCK = 128
REL_BUCKETS = 32
REL_MAX_DIST = 128
RET_HEADS = 4
RET_DK = 128
RET_DV = 128
ROPE_BASE = 10000.0
D_FF = 2816
LN_EPS = 1e-5
GN_EPS = 1e-6
ALPHA = (2.0 * DEPTH) ** 0.25

ATTN_Q = ATTN_HEADS * ATTN_HEAD_DIM
ATTN_KV = ATTN_KV_HEADS * ATTN_HEAD_DIM
RET_QK = RET_HEADS * RET_DK
RET_V = RET_HEADS * RET_DV
C_QA = 0
C_KA = C_QA + ATTN_Q
C_VA = C_KA + ATTN_KV
C_QR = C_VA + ATTN_KV
C_KR = C_QR + RET_QK
C_VR = C_KR + RET_QK
C_GR = C_VR + RET_V
C_GA = C_GR + RET_V
C_GB = C_GA + D_MODEL
IN_COLS = C_GB + D_MODEL
QKV_GROUPS = ((C_QA, C_QR), (C_QR, C_KR), (C_KR, C_VR), (C_VR, C_GR))
GATE_GROUPS = ((C_GR, C_GA), (C_GA, C_GB), (C_GB, IN_COLS))
PROJ_GROUPS = QKV_GROUPS + GATE_GROUPS

FF_CHUNK = 256
N_FF_CHUNKS = D_FF // FF_CHUNK
NEG = -1e30

F32 = jnp.float32
BF16 = jnp.bfloat16

VMEM_LIMIT_BYTES = 56 * 1024 * 1024


def _dot(a, b):
    return jnp.dot(a, b, preferred_element_type=F32)


def _dot_nt(a, b):
    return lax.dot_general(a, b, (((1,), (1,)), ((), ())), preferred_element_type=F32)


def _dot_tn(a, b):
    return lax.dot_general(a, b, (((0,), (0,)), ((), ())), preferred_element_type=F32)


def _layer_norm(z, g, b):
    mu = jnp.mean(z, axis=-1, keepdims=True)
    zc = z - mu
    var = jnp.mean(zc * zc, axis=-1, keepdims=True)
    return zc * lax.rsqrt(var + LN_EPS) * g + b


def _group_norm(o):
    mu = jnp.mean(o, axis=-1, keepdims=True)
    oc = o - mu
    var = jnp.mean(oc * oc, axis=-1, keepdims=True)
    return oc * lax.rsqrt(var + GN_EPS)


def _silu(x):
    return x * jax.nn.sigmoid(x)


def _const_spec(shape):
    nd = len(shape)
    return pl.BlockSpec(shape, lambda *_: (0,) * nd)


def _smem_spec():
    return pl.BlockSpec(memory_space=pltpu.SMEM)


def _ffn_ln_kernel(x_ref, wup_ref, wd_ref, g_ref, b_ref, o_ref, acc_ref):
    x = x_ref[...]
    xb = x.astype(BF16)
    for c in range(N_FF_CHUNKS):
        c0, c1 = c * FF_CHUNK, (c + 1) * FF_CHUNK
        gate = _dot(xb, wup_ref[:, c0:c1])
        up = _dot(xb, wup_ref[:, D_FF + c0:D_FF + c1])
        a = (_silu(gate) * up).astype(BF16)
        d = _dot(a, wd_ref[c0:c1, :])
        if c == 0:
            acc_ref[...] = d
        else:
            acc_ref[...] += d
    z = ALPHA * x + 0.5 * acc_ref[...]
    o_ref[...] = _layer_norm(z, g_ref[...], b_ref[...])


def _ffn_ln(x2d, wup, wd, g, b, tm):
    n = x2d.shape[0]
    assert n % tm == 0
    return pl.pallas_call(
        _ffn_ln_kernel,
        grid=(n // tm,),
        in_specs=[
            pl.BlockSpec((tm, D_MODEL), lambda i: (i, 0)),
            _const_spec(wup.shape),
            _const_spec(wd.shape),
            _const_spec(g.shape),
            _const_spec(b.shape),
        ],
        out_specs=pl.BlockSpec((tm, D_MODEL), lambda i: (i, 0)),
        out_shape=jax.ShapeDtypeStruct((n, D_MODEL), F32),
        scratch_shapes=[pltpu.VMEM((tm, D_MODEL), F32)],
        compiler_params=pltpu.CompilerParams(
            dimension_semantics=("arbitrary",), vmem_limit_bytes=VMEM_LIMIT_BYTES),
        name="ffn_ln",
    )(x2d, wup, wd, g, b)


def _rotary(x, cos_full, sin_signed):
    return x * cos_full + pltpu.roll(x, RET_DK // 2, 1) * sin_signed


def _prompt_mixer_kernel(tile, h_ref, cos_ref, sin_ref, bucket_ref, relb_ref, sinks_ref, dmask_ref,
                         qdec_ref, tail_ref, cdec_ref, w_in_ref, w_ao_ref, w_ro_ref, w_o_ref, g_ref, b_ref,
                         h2_ref, kwin_ref, vwin_ref, sfin_ref,
                         proj_ref, bias_ref, kprev_ref, vprev_ref, s_ref, oattn_ref, r_ref):
    bi = pl.program_id(0)
    ti = pl.program_id(1)

    @pl.when(jnp.logical_and(bi == 0, ti == 0))
    def _build_bias():
        bucket = bucket_ref[...]
        for h in range(ATTN_HEADS):
            def body(k, acc, h=h):
                return jnp.where(bucket == k, relb_ref[k, h], acc)
            bias_ref[h] = lax.fori_loop(0, REL_BUCKETS, body, jnp.full((BLOCK, 2 * BLOCK), NEG, F32))

    @pl.when(ti == 0)
    def _reset_carry():
        kprev_ref[...] = jnp.zeros_like(kprev_ref)
        vprev_ref[...] = jnp.zeros_like(vprev_ref)
        s_ref[...] = jnp.zeros_like(s_ref)

    nsb = tile // BLOCK
    col = lax.broadcasted_iota(jnp.int32, (BLOCK, 2 * BLOCK), 1)

    def rows_of(sb):
        return slice(sb * BLOCK, (sb + 1) * BLOCK)

    def project(sb, groups):
        rows = rows_of(sb)
        hb = h_ref[rows, :].astype(BF16)
        for c0, c1 in groups:
            proj_ref[rows, c0:c1] = _dot(hb, w_in_ref[:, c0:c1])

    def mix_first(sb):
        rows = rows_of(sb)
        ka_cur = proj_ref[rows, C_KA:C_VA]
        va_cur = proj_ref[rows, C_VA:C_QR]
        if sb == 0:
            ka_prev, va_prev = kprev_ref[...], vprev_ref[...]
        else:
            ka_prev = proj_ref[rows_of(sb - 1), C_KA:C_VA]
            va_prev = proj_ref[rows_of(sb - 1), C_VA:C_QR]
        k2 = jnp.concatenate([ka_prev, ka_cur], axis=0)
        v2 = jnp.concatenate([va_prev, va_cur], axis=0)
        probs, vals = [], []
        for kv in range(ATTN_KV_HEADS):
            kk = k2[:, kv * ATTN_HEAD_DIM:(kv + 1) * ATTN_HEAD_DIM].astype(BF16)
            vals.append(v2[:, kv * ATTN_HEAD_DIM:(kv + 1) * ATTN_HEAD_DIM].astype(BF16))
            for g in range(ATTN_GROUP):
                h = kv * ATTN_GROUP + g
                q = proj_ref[rows, h * ATTN_HEAD_DIM:(h + 1) * ATTN_HEAD_DIM].astype(BF16)
                s = _dot_nt(q, kk) * (ATTN_HEAD_DIM ** -0.5) + bias_ref[h]
                if sb == 0:
                    s = jnp.where(jnp.logical_and(ti == 0, col < BLOCK), NEG, s)
                sink = sinks_ref[h]
                m = jnp.maximum(jnp.max(s, axis=-1, keepdims=True), sink)
                p = jnp.exp(s - m)
                denom = jnp.sum(p, axis=-1, keepdims=True) + jnp.exp(sink - m)
                probs.append((p / denom).astype(BF16))
        if sb == nsb - 1:
            kprev_ref[...] = ka_cur
            vprev_ref[...] = va_cur
            kwin_ref[...] = ka_cur
            vwin_ref[...] = va_cur

        cos_b = cos_ref[rows, :]
        sin_b = sin_ref[rows, :]
        scores, vbs, crosses = [], [], []
        for h in range(RET_HEADS):
            qh = _rotary(proj_ref[rows, C_QR + h * RET_DK:C_QR + (h + 1) * RET_DK], cos_b, sin_b)
            kh = _rotary(proj_ref[rows, C_KR + h * RET_DK:C_KR + (h + 1) * RET_DK], cos_b, sin_b) * (RET_DK ** -0.5)
            vb = proj_ref[rows, C_VR + h * RET_DV:C_VR + (h + 1) * RET_DV].astype(BF16)
            scores.append((_dot_nt(qh.astype(BF16), kh.astype(BF16)) * dmask_ref[h]).astype(BF16))
            s_prev = s_ref[h]
            crosses.append(_dot((qh * qdec_ref[h]).astype(BF16), s_prev.astype(BF16)))
            u = _dot_tn((kh * tail_ref[h]).astype(BF16), vb)
            s_ref[h] = cdec_ref[h] * s_prev + u
            vbs.append(vb)
        return probs, vals, scores, vbs, crosses

    def mix_second(sb, operands):
        rows = rows_of(sb)
        probs, vals, scores, vbs, crosses = operands
        for h in range(ATTN_HEADS):
            oattn_ref[rows, h * ATTN_HEAD_DIM:(h + 1) * ATTN_HEAD_DIM] = _dot(probs[h], vals[h // ATTN_GROUP])
        for h in range(RET_HEADS):
            o = _dot(scores[h], vbs[h]) + crosses[h]
            gr = proj_ref[rows, C_GR + h * RET_DV:C_GR + (h + 1) * RET_DV]
            r_ref[rows, h * RET_DV:(h + 1) * RET_DV] = _group_norm(o) * _silu(gr)

    def merge(sb):
        rows = rows_of(sb)
        a_out = _dot(oattn_ref[rows, :].astype(BF16), w_ao_ref[...])
        r_out = _dot(r_ref[rows, :].astype(BF16), w_ro_ref[...])
        merged = (jax.nn.sigmoid(proj_ref[rows, C_GA:C_GB]) * a_out
                  + jax.nn.sigmoid(proj_ref[rows, C_GB:IN_COLS]) * r_out)
        y = _dot(merged.astype(BF16), w_o_ref[...])
        h2_ref[rows, :] = _layer_norm(ALPHA * h_ref[rows, :] + y, g_ref[...], b_ref[...])

    project(0, QKV_GROUPS)
    for sb in range(nsb):
        operands = mix_first(sb)
        project(sb, GATE_GROUPS)
        mix_second(sb, operands)
        if sb + 1 < nsb:
            project(sb + 1, QKV_GROUPS)
        merge(sb)

    @pl.when(ti == pl.num_programs(1) - 1)
    def _emit_state():
        sfin_ref[...] = s_ref[...]


def _prompt_mixer(h, tabs, w_in, w_ao, w_ro, w_o, g, b, relb, sinks, tile):
    bsz, seq, _ = h.shape
    assert seq % tile == 0 and tile % BLOCK == 0
    nt = seq // tile
    kern = functools.partial(_prompt_mixer_kernel, tile)
    out_shape = (
        jax.ShapeDtypeStruct((bsz, seq, D_MODEL), F32),
        jax.ShapeDtypeStruct((bsz, WINDOW, ATTN_KV), F32),
        jax.ShapeDtypeStruct((bsz, WINDOW, ATTN_KV), F32),
        jax.ShapeDtypeStruct((bsz, RET_HEADS, RET_DK, RET_DV), F32),
    )
    return pl.pallas_call(
        kern,
        grid=(bsz, nt),
        in_specs=[
            pl.BlockSpec((None, tile, D_MODEL), lambda i, j: (i, j, 0)),
            pl.BlockSpec((tile, RET_DK), lambda i, j: (j, 0)),
            pl.BlockSpec((tile, RET_DK), lambda i, j: (j, 0)),
            _const_spec(tabs["bucket"].shape),
            _smem_spec(),
            _smem_spec(),
            _const_spec(tabs["dmask"].shape),
            _const_spec(tabs["qdec"].shape),
            _const_spec(tabs["tail"].shape),
            _smem_spec(),
            _const_spec(w_in.shape),
            _const_spec(w_ao.shape),
            _const_spec(w_ro.shape),
            _const_spec(w_o.shape),
            _const_spec(g.shape),
            _const_spec(b.shape),
        ],
        out_specs=(
            pl.BlockSpec((None, tile, D_MODEL), lambda i, j: (i, j, 0)),
            pl.BlockSpec((None, WINDOW, ATTN_KV), lambda i, j: (i, 0, 0)),
            pl.BlockSpec((None, WINDOW, ATTN_KV), lambda i, j: (i, 0, 0)),
            pl.BlockSpec((None, RET_HEADS, RET_DK, RET_DV), lambda i, j: (i, 0, 0, 0)),
        ),
        out_shape=out_shape,
        scratch_shapes=[
            pltpu.VMEM((tile, IN_COLS), F32),
            pltpu.VMEM((ATTN_HEADS, BLOCK, 2 * BLOCK), F32),
            pltpu.VMEM((BLOCK, ATTN_KV), F32),
            pltpu.VMEM((BLOCK, ATTN_KV), F32),
            pltpu.VMEM((RET_HEADS, RET_DK, RET_DV), F32),
            pltpu.VMEM((tile, ATTN_Q), F32),
            pltpu.VMEM((tile, RET_V), F32),
        ],
        compiler_params=pltpu.CompilerParams(
            dimension_semantics=("arbitrary", "arbitrary"), vmem_limit_bytes=VMEM_LIMIT_BYTES),
        name="prompt_mixer",
    )(h, tabs["cos"], tabs["sin"], tabs["bucket"], relb, sinks, tabs["dmask"], tabs["qdec"], tabs["tail"],
      tabs["cdec"], w_in, w_ao, w_ro, w_o, g, b)


def _sample_mixer_kernel(bt, hs_ref, cos_ref, sin_ref, sbucket_ref, relbt_ref, sinkv_ref, gam_ref,
                         w_in_ref, w_ao_ref, w_ro_ref, w_o_ref, g_ref, b_ref, ck_ref, cv_ref, st_ref,
                         ys_ref, kw_ref, vw_ref, stn_ref,
                         proj_ref, sbias_ref, qrot_ref, qdec_ref, kt_ref, vall_ref, knt_ref, vnt_ref,
                         cross_ref, oattn_ref):
    step = pl.program_id(0)
    nb = hs_ref.shape[0]

    @pl.when(step == 0)
    def _project():
        hb = hs_ref[...].astype(BF16)
        for c0, c1 in PROJ_GROUPS:
            proj_ref[:, c0:c1] = _dot(hb, w_in_ref[:, c0:c1])
        bucket = sbucket_ref[...]
        acc = jnp.full((ATTN_HEADS, WINDOW), NEG, F32)
        for k in range(REL_BUCKETS):
            acc = jnp.where(bucket == k, relbt_ref[:, k:k + 1], acc)
        sbias_ref[...] = acc
        cos_b = cos_ref[...]
        sin_b = sin_ref[...]
        for h in range(RET_HEADS):
            hc = slice(h * RET_DK, (h + 1) * RET_DK)
            qh = _rotary(proj_ref[:, C_QR + h * RET_DK:C_QR + (h + 1) * RET_DK], cos_b, sin_b)
            kh = _rotary(proj_ref[:, C_KR + h * RET_DK:C_KR + (h + 1) * RET_DK], cos_b, sin_b) * (RET_DK ** -0.5)
            qrot_ref[:, hc] = qh
            qdec_ref[:, hc] = qh * gam_ref[h]
            proj_ref[:, C_KR + h * RET_DK:C_KR + (h + 1) * RET_DK] = kh
            kt_ref[h] = kh.T.astype(BF16)
            vall_ref[h] = proj_ref[:, C_VR + h * RET_DV:C_VR + (h + 1) * RET_DV].astype(BF16)
        knt_ref[...] = proj_ref[:, C_KA:C_VA].T
        vnt_ref[...] = proj_ref[:, C_VA:C_QR].T

    r0 = pl.multiple_of(step * bt, bt)
    row8 = lax.broadcasted_iota(jnp.int32, (ATTN_HEADS, WINDOW), 0)
    lane8 = lax.broadcasted_iota(jnp.int32, (ATTN_HEADS, WINDOW), 1)
    half = ATTN_HEAD_DIM
    in_kv_half = (row8 < ATTN_GROUP) == (lane8 < half)
    need_roll = (row8 == 1) | (row8 == 3) | (row8 == 4) | (row8 == 6)
    lane_w = lax.broadcasted_iota(jnp.int32, (ATTN_KV, WINDOW), 1)
    rowb = lax.broadcasted_iota(jnp.int32, (bt, RET_DV), 0)
    lane_b = lax.broadcasted_iota(jnp.int32, (RET_DK, nb), 1)
    sinkv = sinkv_ref[...]
    bias_new = relbt_ref[:, 0:1]
    sbias = sbias_ref[...]
    cross_acc = [jnp.zeros((bt, RET_DV), F32) for _ in range(RET_HEADS)]
    oattn_acc = [jnp.zeros((bt, 128), F32) for _ in range(ATTN_Q // 128)]
    qa_tile = [proj_ref[pl.ds(r0, bt), C_QA + k * 128:C_QA + (k + 1) * 128] for k in range(ATTN_Q // 128)]
    ka_tile = proj_ref[pl.ds(r0, bt), C_KA:C_VA]
    va_tile = proj_ref[pl.ds(r0, bt), C_VA:C_QR]

    scale = ATTN_HEAD_DIM ** -0.5
    queries, scores = [], []
    for bl in range(bt):
        qrow = [jnp.broadcast_to(qa_tile[k][bl:bl + 1, :], (ATTN_HEADS, 128)) for k in range(4)]
        gsel = jnp.where(row8 < 2, qrow[0], jnp.where(row8 < 4, qrow[1], jnp.where(row8 < 6, qrow[2], qrow[3])))
        gsel = jnp.where(need_roll, pltpu.roll(gsel, half, 1), gsel)
        qb8 = jnp.where(in_kv_half, gsel, 0.0).astype(BF16)
        queries.append(qb8)
        scores.append(_dot(qb8, ck_ref[bl].astype(BF16)) * scale + sbias)

    for bl in range(bt):
        bg = r0 + bl
        for h in range(RET_HEADS):
            hc = slice(h * RET_DK, (h + 1) * RET_DK)
            s_prev = st_ref[bl, h]
            qd = qdec_ref[pl.ds(r0, bt), hc].astype(BF16)
            res = _dot(qd, s_prev.astype(BF16))
            cross_acc[h] = jnp.where(rowb == bl, res, cross_acc[h])
            k_col = jnp.where(lane_b == bg, kt_ref[h], jnp.zeros_like(kt_ref[h]))
            u = _dot(k_col, vall_ref[h])
            stn_ref[bl, h] = gam_ref[h] * s_prev + u

    for bl in range(bt):
        bg = r0 + bl
        qb8 = queries[bl]
        s = scores[bl]
        knew = ka_tile[bl:bl + 1, :]
        vnew = va_tile[bl:bl + 1, :]
        s_new = jnp.sum(qb8.astype(F32) * knew.astype(BF16).astype(F32), axis=-1, keepdims=True) * scale + bias_new
        m = jnp.maximum(jnp.maximum(jnp.max(s, axis=-1, keepdims=True), s_new), sinkv)
        p = jnp.exp(s - m)
        p_new = jnp.exp(s_new - m)
        denom = jnp.sum(p, axis=-1, keepdims=True) + p_new + jnp.exp(sinkv - m)
        p = p / denom
        p_new = p_new / denom
        vt = cv_ref[bl]
        o8 = _dot_nt(p.astype(BF16), vt.astype(BF16)) + p_new.astype(BF16).astype(F32) * vnew.astype(BF16).astype(F32)
        o8 = jnp.where(need_roll, pltpu.roll(o8, half, 1), o8)
        o8 = jnp.where(((row8 % 2) == 0) == (lane8 < half), o8, 0.0)
        for k in range(4):
            pair = o8[2 * k:2 * k + 1, :] + o8[2 * k + 1:2 * k + 2, :]
            oattn_acc[k] = jnp.where(rowb == bl, pair, oattn_acc[k])
        shift = (WINDOW - 1) - bg
        kw_ref[bl] = jnp.where(lane_w == WINDOW - 1, pltpu.roll(knt_ref[...], shift, 1),
                               pltpu.roll(ck_ref[bl], WINDOW - 1, 1))
        vw_ref[bl] = jnp.where(lane_w == WINDOW - 1, pltpu.roll(vnt_ref[...], shift, 1),
                               pltpu.roll(vt, WINDOW - 1, 1))
    for h in range(RET_HEADS):
        cross_ref[pl.ds(r0, bt), h * RET_DV:(h + 1) * RET_DV] = cross_acc[h]
    for k in range(ATTN_Q // 128):
        oattn_ref[pl.ds(r0, bt), k * 128:(k + 1) * 128] = oattn_acc[k]

    @pl.when(step == pl.num_programs(0) - 1)
    def _merge():
        for h in range(RET_HEADS):
            hc = slice(h * RET_DK, (h + 1) * RET_DK)
            qf = qrot_ref[:, hc].astype(BF16).astype(F32)
            kf = proj_ref[:, C_KR + h * RET_DK:C_KR + (h + 1) * RET_DK].astype(BF16).astype(F32)
            vf = proj_ref[:, C_VR + h * RET_DV:C_VR + (h + 1) * RET_DV].astype(BF16).astype(F32)
            score = jnp.sum(qf * kf, axis=-1, keepdims=True)
            o = score.astype(BF16).astype(F32) * vf + cross_ref[:, hc]
            gr = proj_ref[:, C_GR + h * RET_DV:C_GR + (h + 1) * RET_DV]
            cross_ref[:, hc] = _group_norm(o) * _silu(gr)
        a_out = _dot(oattn_ref[...].astype(BF16), w_ao_ref[...])
        r_out = _dot(cross_ref[...].astype(BF16), w_ro_ref[...])
        merged = (jax.nn.sigmoid(proj_ref[:, C_GA:C_GB]) * a_out
                  + jax.nn.sigmoid(proj_ref[:, C_GB:IN_COLS]) * r_out)
        y = _dot(merged.astype(BF16), w_o_ref[...])
        ys_ref[...] = _layer_norm(ALPHA * hs_ref[...] + y, g_ref[...], b_ref[...])


def _sample_mixer(hs, ck, cv, st, tabs, w_in, w_ao, w_ro, w_o, g, b, relbt, sinkv, bt):
    nb = hs.shape[0]
    assert nb % bt == 0
    kern = functools.partial(_sample_mixer_kernel, bt)
    out_shape = (
        jax.ShapeDtypeStruct((nb, D_MODEL), F32),
        jax.ShapeDtypeStruct((nb, ATTN_KV, WINDOW), F32),
        jax.ShapeDtypeStruct((nb, ATTN_KV, WINDOW), F32),
        jax.ShapeDtypeStruct((nb, RET_HEADS, RET_DK, RET_DV), F32),
    )
    cache_spec = pl.BlockSpec((bt, ATTN_KV, WINDOW), lambda i: (i, 0, 0))
    return pl.pallas_call(
        kern,
        grid=(nb // bt,),
        in_specs=[
            _const_spec(hs.shape),
            _const_spec(tabs["cos_s"].shape),
            _const_spec(tabs["sin_s"].shape),
            _const_spec(tabs["sbucket"].shape),
            _const_spec(relbt.shape),
            _const_spec(sinkv.shape),
            _smem_spec(),
            _const_spec(w_in.shape),
            _const_spec(w_ao.shape),
            _const_spec(w_ro.shape),
            _const_spec(w_o.shape),
            _const_spec(g.shape),
            _const_spec(b.shape),
            cache_spec,
            cache_spec,
            pl.BlockSpec((bt, RET_HEADS, RET_DK, RET_DV), lambda i: (i, 0, 0, 0)),
        ],
        out_specs=(
            _const_spec((nb, D_MODEL)),
            cache_spec,
            cache_spec,
            pl.BlockSpec((bt, RET_HEADS, RET_DK, RET_DV), lambda i: (i, 0, 0, 0)),
        ),
        out_shape=out_shape,
        scratch_shapes=[
            pltpu.VMEM((nb, IN_COLS), F32),
            pltpu.VMEM((ATTN_HEADS, WINDOW), F32),
            pltpu.VMEM((nb, RET_QK), F32),
            pltpu.VMEM((nb, RET_QK), F32),
            pltpu.VMEM((RET_HEADS, RET_DK, nb), BF16),
            pltpu.VMEM((RET_HEADS, nb, RET_DV), BF16),
            pltpu.VMEM((ATTN_KV, nb), F32),
            pltpu.VMEM((ATTN_KV, nb), F32),
            pltpu.VMEM((nb, RET_V), F32),
            pltpu.VMEM((nb, ATTN_Q), F32),
        ],
        compiler_params=pltpu.CompilerParams(
            dimension_semantics=("arbitrary",), vmem_limit_bytes=VMEM_LIMIT_BYTES),
        name="sample_mixer",
    )(hs, tabs["cos_s"], tabs["sin_s"], tabs["sbucket"], relbt, sinkv, tabs["gamma"],
      w_in, w_ao, w_ro, w_o, g, b, ck, cv, st)


def _rel_bucket_np(dist):
    n = np.maximum(dist, 0)
    max_exact = REL_BUCKETS // 2
    ratio = np.maximum(n, 1).astype(np.float32) / np.float32(max_exact)
    large = max_exact + (np.log(np.maximum(ratio, np.float32(1.0))) / np.float32(math.log(REL_MAX_DIST / max_exact))
                         * np.float32(REL_BUCKETS - max_exact)).astype(np.int32)
    large = np.minimum(large, REL_BUCKETS - 1)
    return np.where(n < max_exact, n, large).astype(np.int32)


def _tables(seq):
    half = RET_DK // 2
    inv = ROPE_BASE ** (-jnp.arange(half, dtype=F32) / half)

    def rope(pos):
        ang = pos.astype(F32)[:, None] * inv[None, :]
        cos, sin = jnp.cos(ang), jnp.sin(ang)
        return jnp.concatenate([cos, cos], axis=-1), jnp.concatenate([-sin, sin], axis=-1)

    cos_p, sin_p = rope(jnp.arange(seq, dtype=jnp.int32))
    cos_s, sin_s = rope(PAST_LEN + jnp.arange(1, dtype=jnp.int32))

    qi = np.arange(BLOCK)[:, None]
    kj = np.arange(2 * BLOCK)[None, :]
    dist = BLOCK + qi - kj
    allowed = (dist >= 0) & (dist < WINDOW)
    bucket = np.where(allowed, _rel_bucket_np(dist), -1).astype(np.int32)
    sdist = WINDOW - np.arange(WINDOW)
    sbucket = np.where(sdist < WINDOW, _rel_bucket_np(sdist), -1).astype(np.int32)
    sbucket = np.broadcast_to(sbucket[None, :], (ATTN_HEADS, WINDOW))

    lg = jnp.log1p(-(2.0 ** (-5.0 - jnp.arange(RET_HEADS, dtype=F32))))
    i = jnp.arange(BLOCK, dtype=F32)
    diff = i[:, None] - i[None, :]
    dmask = jnp.where(diff[None] >= 0, jnp.exp(jnp.maximum(diff, 0.0)[None] * lg[:, None, None]), 0.0)
    ones = jnp.ones((RET_HEADS, BLOCK, RET_DK), F32)
    qdec = jnp.exp((i + 1)[None, :] * lg[:, None])[:, :, None] * ones
    tail = jnp.exp((BLOCK - 1 - i)[None, :] * lg[:, None])[:, :, None] * ones
    return {
        "cos": cos_p, "sin": sin_p, "cos_s": cos_s, "sin_s": sin_s,
        "bucket": jnp.asarray(bucket), "sbucket": jnp.asarray(sbucket),
        "dmask": dmask, "qdec": qdec, "tail": tail,
        "cdec": jnp.exp(BLOCK * lg), "gamma": jnp.exp(lg),
    }


def _ffn_weights(w_up, w_down):
    return w_up.astype(BF16), w_down.astype(BF16)


FFN_TILE = 512
MIX_TILE = 512
SAMPLE_BT = 8


def kernel(x_prompt, x_sample, cache_k_win, cache_v_win, state_ret, rel_bias, w_in, attn_sinks, w_attn_out,
           w_ret_out, w_o, ffn1_w_up, ffn1_w_down, ffn2_w_up, ffn2_w_down, ln1_g, ln1_b, ln2_g, ln2_b, ln3_g, ln3_b):
    bsz, seq, _ = x_prompt.shape
    nb = x_sample.shape[0]
    tabs = _tables(seq)
    hp = x_prompt.reshape(bsz * seq, D_MODEL)
    hs = x_sample.reshape(nb, D_MODEL)
    outs = [[] for _ in range(6)]
    row = lambda v: v.reshape(1, D_MODEL)
    for l in range(DEPTH):
        f1 = _ffn_weights(ffn1_w_up[l], ffn1_w_down[l])
        f2 = _ffn_weights(ffn2_w_up[l], ffn2_w_down[l])
        w_in_b = w_in[l].astype(BF16)
        w_ao_b = w_attn_out[l].astype(BF16)
        w_ro_b = w_ret_out[l].astype(BF16)
        w_o_b = w_o[l].astype(BF16)
        sinks = attn_sinks[l]

        hp = _ffn_ln(hp, *f1, row(ln1_g[l]), row(ln1_b[l]), FFN_TILE)
        hs = _ffn_ln(hs, *f1, row(ln1_g[l]), row(ln1_b[l]), nb)

        hp3, kp, vp, sp = _prompt_mixer(hp.reshape(bsz, seq, D_MODEL), tabs, w_in_b, w_ao_b, w_ro_b, w_o_b,
                                        row(ln2_g[l]), row(ln2_b[l]), rel_bias, sinks, MIX_TILE)
        to_t = lambda c: c.transpose(0, 2, 3, 1).reshape(nb, ATTN_KV, WINDOW)
        from_t = lambda c: c.reshape(nb, ATTN_KV_HEADS, ATTN_HEAD_DIM, WINDOW).transpose(0, 3, 1, 2)
        hs, ks, vs, ss = _sample_mixer(
            hs, to_t(cache_k_win[l]), to_t(cache_v_win[l]),
            state_ret[l], tabs, w_in_b, w_ao_b, w_ro_b, w_o_b, row(ln2_g[l]), row(ln2_b[l]),
            rel_bias.T, sinks.reshape(ATTN_HEADS, 1), SAMPLE_BT)
        ks, vs = from_t(ks), from_t(vs)
        hp = hp3.reshape(bsz * seq, D_MODEL)

        hp = _ffn_ln(hp, *f2, row(ln3_g[l]), row(ln3_b[l]), FFN_TILE)
        hs = _ffn_ln(hs, *f2, row(ln3_g[l]), row(ln3_b[l]), nb)

        kv_shape = (WINDOW, ATTN_KV_HEADS, ATTN_HEAD_DIM)
        for acc, v in zip(outs, (kp.reshape(bsz, *kv_shape), vp.reshape(bsz, *kv_shape), sp, ks, vs, ss)):
            acc.append(v)
    stacked = [jnp.stack(v) for v in outs]
    return (hp.reshape(bsz, seq, D_MODEL), hs.reshape(nb, 1, D_MODEL), *stacked)
```

```python
import functools
import math

import numpy as np
import jax
import jax.numpy as jnp
from jax import lax
from jax.experimental import pallas as pl
from jax.experimental.pallas import tpu as pltpu

D_MODEL = 1024
DEPTH = 1
PAST_LEN = 16384
ATTN_HEADS = 8
ATTN_KV_HEADS = 2
ATTN_HEAD_DIM = 64
ATTN_GROUP = ATTN_HEADS // ATTN_KV_HEADS
WINDOW = 128
BLOCK = 128
REL_BUCKETS = 32
REL_MAX_DIST = 128
RET_HEADS = 4
RET_DK = 128
RET_DV = 128
ROPE_BASE = 10000.0
D_FF = 2816
LN_EPS = 1e-5
GN_EPS = 1e-6
ALPHA = (2.0 * DEPTH) ** 0.25

ATTN_Q = ATTN_HEADS * ATTN_HEAD_DIM
ATTN_KV = ATTN_KV_HEADS * ATTN_HEAD_DIM
RET_QK = RET_HEADS * RET_DK
RET_V = RET_HEADS * RET_DV
C_QA = 0
C_KA = C_QA + ATTN_Q
C_VA = C_KA + ATTN_KV
C_QR = C_VA + ATTN_KV
C_KR = C_QR + RET_QK
C_VR = C_KR + RET_QK
C_GR = C_VR + RET_V
C_GA = C_GR + RET_V
C_GB = C_GA + D_MODEL
IN_COLS = C_GB + D_MODEL
QKV_GROUPS = ((C_QA, C_QR), (C_QR, C_KR), (C_KR, C_VR), (C_VR, C_GR))
GATE_GROUPS = ((C_GR, C_GA), (C_GA, C_GB), (C_GB, IN_COLS))
PROJ_GROUPS = QKV_GROUPS + GATE_GROUPS

FF_CHUNK = 256
N_FF_CHUNKS = D_FF // FF_CHUNK
NEG = -1e30

F32 = jnp.float32
BF16 = jnp.bfloat16

VMEM_LIMIT_BYTES = 56 * 1024 * 1024


def _dot(a, b):
    return jnp.dot(a, b, preferred_element_type=F32)


def _dot_nt(a, b):
    return lax.dot_general(a, b, (((1,), (1,)), ((), ())), preferred_element_type=F32)


def _dot_tn(a, b):
    return lax.dot_general(a, b, (((0,), (0,)), ((), ())), preferred_element_type=F32)


def _layer_norm(z, g, b):
    mu = jnp.mean(z, axis=-1, keepdims=True)
    zc = z - mu
    var = jnp.mean(zc * zc, axis=-1, keepdims=True)
    return zc * lax.rsqrt(var + LN_EPS) * g + b


def _group_norm(o):
    mu = jnp.mean(o, axis=-1, keepdims=True)
    oc = o - mu
    var = jnp.mean(oc * oc, axis=-1, keepdims=True)
    return oc * lax.rsqrt(var + GN_EPS)


def _silu(x):
    return x * jax.nn.sigmoid(x)


def _const_spec(shape):
    nd = len(shape)
    return pl.BlockSpec(shape, lambda *_: (0,) * nd)


def _resident_spec(shape):
    nd = len(shape)
    return pl.BlockSpec(shape, lambda *_: (0,) * nd, pipeline_mode=pl.Buffered(1))


def _smem_spec():
    return pl.BlockSpec(memory_space=pltpu.SMEM)


def _ffn_ln_kernel(sub, x_ref, wup_ref, wd_ref, g_ref, b_ref, o_ref, act_ref):
    for r0 in range(0, x_ref.shape[0], sub):
        rows = slice(r0, r0 + sub)
        x = x_ref[rows, :]
        xb = x.astype(BF16)
        for c in range(N_FF_CHUNKS):
            c0, c1 = c * FF_CHUNK, (c + 1) * FF_CHUNK
            gate = _dot(xb, wup_ref[:, c0:c1])
            up = _dot(xb, wup_ref[:, D_FF + c0:D_FF + c1])
            act_ref[rows, c0:c1] = (_silu(gate) * up).astype(BF16)
        z = ALPHA * x + 0.5 * _dot(act_ref[rows, :], wd_ref[...])
        o_ref[rows, :] = _layer_norm(z, g_ref[...], b_ref[...])


def _ffn_ln(x2d, wup, wd, g, b, tm, sub):
    n = x2d.shape[0]
    assert n % tm == 0 and tm % sub == 0
    return pl.pallas_call(
        functools.partial(_ffn_ln_kernel, sub),
        grid=(n // tm,),
        in_specs=[
            pl.BlockSpec((tm, D_MODEL), lambda i: (i, 0)),
            _resident_spec(wup.shape),
            _resident_spec(wd.shape),
            _const_spec(g.shape),
            _const_spec(b.shape),
        ],
        out_specs=pl.BlockSpec((tm, D_MODEL), lambda i: (i, 0)),
        out_shape=jax.ShapeDtypeStruct((n, D_MODEL), F32),
        scratch_shapes=[pltpu.VMEM((tm, D_FF), BF16)],
        compiler_params=pltpu.CompilerParams(
            dimension_semantics=("arbitrary",), vmem_limit_bytes=VMEM_LIMIT_BYTES),
        name="ffn_ln",
    )(x2d, wup, wd, g, b)


def _rotary(x, cos_full, sin_signed):
    return x * cos_full + pltpu.roll(x, RET_DK // 2, 1) * sin_signed


def _prompt_mixer_kernel(tile, h_ref, cos_ref, sin_ref, bucket_ref, relb_ref, sinks_ref, dmask_ref,
                         qdec_ref, tail_ref, cdec_ref, w_in_ref, w_ao_ref, w_ro_ref, w_o_ref, g_ref, b_ref,
                         h2_ref, kwin_ref, vwin_ref, sfin_ref,
                         proj_ref, bias_ref, kprev_ref, vprev_ref, s_ref, oattn_ref, r_ref):
    bi = pl.program_id(0)
    ti = pl.program_id(1)

    @pl.when(jnp.logical_and(bi == 0, ti == 0))
    def _build_bias():
        bucket = bucket_ref[...]
        for h in range(ATTN_HEADS):
            def body(k, acc, h=h):
                return jnp.where(bucket == k, relb_ref[k, h], acc)
            bias_ref[h] = lax.fori_loop(0, REL_BUCKETS, body, jnp.full((BLOCK, 2 * BLOCK), NEG, F32))

    @pl.when(ti == 0)
    def _reset_carry():
        kprev_ref[...] = jnp.zeros_like(kprev_ref)
        vprev_ref[...] = jnp.zeros_like(vprev_ref)
        s_ref[...] = jnp.zeros_like(s_ref)

    nsb = tile // BLOCK
    col = lax.broadcasted_iota(jnp.int32, (BLOCK, 2 * BLOCK), 1)

    def rows_of(sb):
        return slice(sb * BLOCK, (sb + 1) * BLOCK)

    def project(sb, groups):
        rows = rows_of(sb)
        hb = h_ref[rows, :].astype(BF16)
        for c0, c1 in groups:
            proj_ref[rows, c0:c1] = _dot(hb, w_in_ref[:, c0:c1])

    def mix_first(sb):
        rows = rows_of(sb)
        ka_cur = proj_ref[rows, C_KA:C_VA]
        va_cur = proj_ref[rows, C_VA:C_QR]
        if sb == 0:
            ka_prev, va_prev = kprev_ref[...], vprev_ref[...]
        else:
            ka_prev = proj_ref[rows_of(sb - 1), C_KA:C_VA]
            va_prev = proj_ref[rows_of(sb - 1), C_VA:C_QR]
        k2 = jnp.concatenate([ka_prev, ka_cur], axis=0)
        v2 = jnp.concatenate([va_prev, va_cur], axis=0)
        probs, vals = [], []
        for kv in range(ATTN_KV_HEADS):
            kk = k2[:, kv * ATTN_HEAD_DIM:(kv + 1) * ATTN_HEAD_DIM].astype(BF16)
            vals.append(v2[:, kv * ATTN_HEAD_DIM:(kv + 1) * ATTN_HEAD_DIM].astype(BF16))
            for g in range(ATTN_GROUP):
                h = kv * ATTN_GROUP + g
                q = proj_ref[rows, h * ATTN_HEAD_DIM:(h + 1) * ATTN_HEAD_DIM].astype(BF16)
                s = _dot_nt(q, kk) * (ATTN_HEAD_DIM ** -0.5) + bias_ref[h]
                if sb == 0:
                    s = jnp.where(jnp.logical_and(ti == 0, col < BLOCK), NEG, s)
                sink = sinks_ref[h]
                m = jnp.maximum(jnp.max(s, axis=-1, keepdims=True), sink)
                p = jnp.exp(s - m)
                denom = jnp.sum(p, axis=-1, keepdims=True) + jnp.exp(sink - m)
                probs.append((p / denom).astype(BF16))
        if sb == nsb - 1:
            kprev_ref[...] = ka_cur
            vprev_ref[...] = va_cur
            kwin_ref[...] = ka_cur
            vwin_ref[...] = va_cur

        cos_b = cos_ref[rows, :]
        sin_b = sin_ref[rows, :]
        scores, vbs, crosses = [], [], []
        for h in range(RET_HEADS):
            qh = _rotary(proj_ref[rows, C_QR + h * RET_DK:C_QR + (h + 1) * RET_DK], cos_b, sin_b)
            kh = _rotary(proj_ref[rows, C_KR + h * RET_DK:C_KR + (h + 1) * RET_DK], cos_b, sin_b) * (RET_DK ** -0.5)
            vb = proj_ref[rows, C_VR + h * RET_DV:C_VR + (h + 1) * RET_DV].astype(BF16)
            scores.append((_dot_nt(qh.astype(BF16), kh.astype(BF16)) * dmask_ref[h]).astype(BF16))
            s_prev = s_ref[h]
            crosses.append(_dot((qh * qdec_ref[h]).astype(BF16), s_prev.astype(BF16)))
            u = _dot_tn((kh * tail_ref[h]).astype(BF16), vb)
            s_ref[h] = cdec_ref[h] * s_prev + u
            vbs.append(vb)
        return probs, vals, scores, vbs, crosses

    def mix_second(sb, operands):
        rows = rows_of(sb)
        probs, vals, scores, vbs, crosses = operands
        for h in range(ATTN_HEADS):
            oattn_ref[rows, h * ATTN_HEAD_DIM:(h + 1) * ATTN_HEAD_DIM] = _dot(probs[h], vals[h // ATTN_GROUP])
        for h in range(RET_HEADS):
            o = _dot(scores[h], vbs[h]) + crosses[h]
            gr = proj_ref[rows, C_GR + h * RET_DV:C_GR + (h + 1) * RET_DV]
            r_ref[rows, h * RET_DV:(h + 1) * RET_DV] = _group_norm(o) * _silu(gr)

    def merge(sb):
        rows = rows_of(sb)
        a_out = _dot(oattn_ref[rows, :].astype(BF16), w_ao_ref[...])
        r_out = _dot(r_ref[rows, :].astype(BF16), w_ro_ref[...])
        merged = (jax.nn.sigmoid(proj_ref[rows, C_GA:C_GB]) * a_out
                  + jax.nn.sigmoid(proj_ref[rows, C_GB:IN_COLS]) * r_out)
        y = _dot(merged.astype(BF16), w_o_ref[...])
        h2_ref[rows, :] = _layer_norm(ALPHA * h_ref[rows, :] + y, g_ref[...], b_ref[...])

    project(0, QKV_GROUPS)
    for sb in range(nsb):
        operands = mix_first(sb)
        project(sb, GATE_GROUPS)
        mix_second(sb, operands)
        if sb + 1 < nsb:
            project(sb + 1, QKV_GROUPS)
        merge(sb)

    @pl.when(ti == pl.num_programs(1) - 1)
    def _emit_state():
        sfin_ref[...] = s_ref[...]


def _prompt_mixer(h, tabs, w_in, w_ao, w_ro, w_o, g, b, relb, sinks, tile):
    bsz, seq, _ = h.shape
    assert seq % tile == 0 and tile % BLOCK == 0
    nt = seq // tile
    kern = functools.partial(_prompt_mixer_kernel, tile)
    out_shape = (
        jax.ShapeDtypeStruct((bsz, seq, D_MODEL), F32),
        jax.ShapeDtypeStruct((bsz, WINDOW, ATTN_KV), F32),
        jax.ShapeDtypeStruct((bsz, WINDOW, ATTN_KV), F32),
        jax.ShapeDtypeStruct((bsz, RET_HEADS, RET_DK, RET_DV), F32),
    )
    return pl.pallas_call(
        kern,
        grid=(bsz, nt),
        in_specs=[
            pl.BlockSpec((None, tile, D_MODEL), lambda i, j: (i, j, 0)),
            pl.BlockSpec((tile, RET_DK), lambda i, j: (j, 0)),
            pl.BlockSpec((tile, RET_DK), lambda i, j: (j, 0)),
            _const_spec(tabs["bucket"].shape),
            _smem_spec(),
            _smem_spec(),
            _const_spec(tabs["dmask"].shape),
            _const_spec(tabs["qdec"].shape),
            _const_spec(tabs["tail"].shape),
            _smem_spec(),
            _const_spec(w_in.shape),
            _const_spec(w_ao.shape),
            _const_spec(w_ro.shape),
            _const_spec(w_o.shape),
            _const_spec(g.shape),
            _const_spec(b.shape),
        ],
        out_specs=(
            pl.BlockSpec((None, tile, D_MODEL), lambda i, j: (i, j, 0)),
            pl.BlockSpec((None, WINDOW, ATTN_KV), lambda i, j: (i, 0, 0)),
            pl.BlockSpec((None, WINDOW, ATTN_KV), lambda i, j: (i, 0, 0)),
            pl.BlockSpec((None, RET_HEADS, RET_DK, RET_DV), lambda i, j: (i, 0, 0, 0)),
        ),
        out_shape=out_shape,
        scratch_shapes=[
            pltpu.VMEM((tile, IN_COLS), F32),
            pltpu.VMEM((ATTN_HEADS, BLOCK, 2 * BLOCK), F32),
            pltpu.VMEM((BLOCK, ATTN_KV), F32),
            pltpu.VMEM((BLOCK, ATTN_KV), F32),
            pltpu.VMEM((RET_HEADS, RET_DK, RET_DV), F32),
            pltpu.VMEM((tile, ATTN_Q), F32),
            pltpu.VMEM((tile, RET_V), F32),
        ],
        compiler_params=pltpu.CompilerParams(
            dimension_semantics=("arbitrary", "arbitrary"), vmem_limit_bytes=VMEM_LIMIT_BYTES),
        name="prompt_mixer",
    )(h, tabs["cos"], tabs["sin"], tabs["bucket"], relb, sinks, tabs["dmask"], tabs["qdec"], tabs["tail"],
      tabs["cdec"], w_in, w_ao, w_ro, w_o, g, b)


def _sample_mixer_kernel(bt, hs_ref, cos_ref, sin_ref, sbucket_ref, relbt_ref, sinkv_ref, gam_ref,
                         w_in_ref, w_ao_ref, w_ro_ref, w_o_ref, g_ref, b_ref, ck_ref, cv_ref, st_ref,
                         ys_ref, kw_ref, vw_ref, stn_ref,
                         proj_ref, sbias_ref, qrot_ref, qdec_ref, kt_ref, vall_ref, knt_ref, vnt_ref,
                         cross_ref, oattn_ref):
    step = pl.program_id(0)
    nb = hs_ref.shape[0]

    @pl.when(step == 0)
    def _project():
        hb = hs_ref[...].astype(BF16)
        for c0, c1 in PROJ_GROUPS:
            proj_ref[:, c0:c1] = _dot(hb, w_in_ref[:, c0:c1])
        bucket = sbucket_ref[...]
        acc = jnp.full((ATTN_HEADS, WINDOW), NEG, F32)
        for k in range(REL_BUCKETS):
            acc = jnp.where(bucket == k, relbt_ref[:, k:k + 1], acc)
        sbias_ref[...] = acc
        cos_b = cos_ref[...]
        sin_b = sin_ref[...]
        for h in range(RET_HEADS):
            hc = slice(h * RET_DK, (h + 1) * RET_DK)
            qh = _rotary(proj_ref[:, C_QR + h * RET_DK:C_QR + (h + 1) * RET_DK], cos_b, sin_b)
            kh = _rotary(proj_ref[:, C_KR + h * RET_DK:C_KR + (h + 1) * RET_DK], cos_b, sin_b) * (RET_DK ** -0.5)
            qrot_ref[:, hc] = qh
            qdec_ref[:, hc] = qh * gam_ref[h]
            proj_ref[:, C_KR + h * RET_DK:C_KR + (h + 1) * RET_DK] = kh
            kt_ref[h] = kh.T.astype(BF16)
            vall_ref[h] = proj_ref[:, C_VR + h * RET_DV:C_VR + (h + 1) * RET_DV].astype(BF16)
        knt_ref[...] = proj_ref[:, C_KA:C_VA].T
        vnt_ref[...] = proj_ref[:, C_VA:C_QR].T

    r0 = pl.multiple_of(step * bt, bt)
    row8 = lax.broadcasted_iota(jnp.int32, (ATTN_HEADS, WINDOW), 0)
    lane8 = lax.broadcasted_iota(jnp.int32, (ATTN_HEADS, WINDOW), 1)
    half = ATTN_HEAD_DIM
    in_kv_half = (row8 < ATTN_GROUP) == (lane8 < half)
    need_roll = (row8 == 1) | (row8 == 3) | (row8 == 4) | (row8 == 6)
    lane_w = lax.broadcasted_iota(jnp.int32, (ATTN_KV, WINDOW), 1)
    rowb = lax.broadcasted_iota(jnp.int32, (bt, RET_DV), 0)
    lane_b = lax.broadcasted_iota(jnp.int32, (RET_DK, nb), 1)
    sinkv = sinkv_ref[...]
    bias_new = relbt_ref[:, 0:1]
    sbias = sbias_ref[...]
    cross_acc = [jnp.zeros((bt, RET_DV), F32) for _ in range(RET_HEADS)]
    oattn_acc = [jnp.zeros((bt, 128), F32) for _ in range(ATTN_Q // 128)]
    qa_tile = [proj_ref[pl.ds(r0, bt), C_QA + k * 128:C_QA + (k + 1) * 128] for k in range(ATTN_Q // 128)]
    ka_tile = proj_ref[pl.ds(r0, bt), C_KA:C_VA]
    va_tile = proj_ref[pl.ds(r0, bt), C_VA:C_QR]

    scale = ATTN_HEAD_DIM ** -0.5
    queries, scores = [], []
    for bl in range(bt):
        qrow = [jnp.broadcast_to(qa_tile[k][bl:bl + 1, :], (ATTN_HEADS, 128)) for k in range(4)]
        gsel = jnp.where(row8 < 2, qrow[0], jnp.where(row8 < 4, qrow[1], jnp.where(row8 < 6, qrow[2], qrow[3])))
        gsel = jnp.where(need_roll, pltpu.roll(gsel, half, 1), gsel)
        qb8 = jnp.where(in_kv_half, gsel, 0.0).astype(BF16)
        queries.append(qb8)
        scores.append(_dot(qb8, ck_ref[bl].astype(BF16)) * scale + sbias)

    for bl in range(bt):
        bg = r0 + bl
        for h in range(RET_HEADS):
            hc = slice(h * RET_DK, (h + 1) * RET_DK)
            s_prev = st_ref[bl, h]
            qd = qdec_ref[pl.ds(r0, bt), hc].astype(BF16)
            res = _dot(qd, s_prev.astype(BF16))
            cross_acc[h] = jnp.where(rowb == bl, res, cross_acc[h])
            k_col = jnp.where(lane_b == bg, kt_ref[h], jnp.zeros_like(kt_ref[h]))
            u = _dot(k_col, vall_ref[h])
            stn_ref[bl, h] = gam_ref[h] * s_prev + u

    for bl in range(bt):
        bg = r0 + bl
        qb8 = queries[bl]
        s = scores[bl]
        knew = ka_tile[bl:bl + 1, :]
        vnew = va_tile[bl:bl + 1, :]
        s_new = jnp.sum(qb8.astype(F32) * knew.astype(BF16).astype(F32), axis=-1, keepdims=True) * scale + bias_new
        m = jnp.maximum(jnp.maximum(jnp.max(s, axis=-1, keepdims=True), s_new), sinkv)
        p = jnp.exp(s - m)
        p_new = jnp.exp(s_new - m)
        denom = jnp.sum(p, axis=-1, keepdims=True) + p_new + jnp.exp(sinkv - m)
        p = p / denom
        p_new = p_new / denom
        vt = cv_ref[bl]
        o8 = _dot_nt(p.astype(BF16), vt.astype(BF16)) + p_new.astype(BF16).astype(F32) * vnew.astype(BF16).astype(F32)
        o8 = jnp.where(need_roll, pltpu.roll(o8, half, 1), o8)
        o8 = jnp.where(((row8 % 2) == 0) == (lane8 < half), o8, 0.0)
        for k in range(4):
            pair = o8[2 * k:2 * k + 1, :] + o8[2 * k + 1:2 * k + 2, :]
            oattn_acc[k] = jnp.where(rowb == bl, pair, oattn_acc[k])
        shift = (WINDOW - 1) - bg
        kw_ref[bl] = jnp.where(lane_w == WINDOW - 1, pltpu.roll(knt_ref[...], shift, 1),
                               pltpu.roll(ck_ref[bl], WINDOW - 1, 1))
        vw_ref[bl] = jnp.where(lane_w == WINDOW - 1, pltpu.roll(vnt_ref[...], shift, 1),
                               pltpu.roll(vt, WINDOW - 1, 1))
    for h in range(RET_HEADS):
        cross_ref[pl.ds(r0, bt), h * RET_DV:(h + 1) * RET_DV] = cross_acc[h]
    for k in range(ATTN_Q // 128):
        oattn_ref[pl.ds(r0, bt), k * 128:(k + 1) * 128] = oattn_acc[k]

    @pl.when(step == pl.num_programs(0) - 1)
    def _merge():
        for h in range(RET_HEADS):
            hc = slice(h * RET_DK, (h + 1) * RET_DK)
            qf = qrot_ref[:, hc].astype(BF16).astype(F32)
            kf = proj_ref[:, C_KR + h * RET_DK:C_KR + (h + 1) * RET_DK].astype(BF16).astype(F32)
            vf = proj_ref[:, C_VR + h * RET_DV:C_VR + (h + 1) * RET_DV].astype(BF16).astype(F32)
            score = jnp.sum(qf * kf, axis=-1, keepdims=True)
            o = score.astype(BF16).astype(F32) * vf + cross_ref[:, hc]
            gr = proj_ref[:, C_GR + h * RET_DV:C_GR + (h + 1) * RET_DV]
            cross_ref[:, hc] = _group_norm(o) * _silu(gr)
        a_out = _dot(oattn_ref[...].astype(BF16), w_ao_ref[...])
        r_out = _dot(cross_ref[...].astype(BF16), w_ro_ref[...])
        merged = (jax.nn.sigmoid(proj_ref[:, C_GA:C_GB]) * a_out
                  + jax.nn.sigmoid(proj_ref[:, C_GB:IN_COLS]) * r_out)
        y = _dot(merged.astype(BF16), w_o_ref[...])
        ys_ref[...] = _layer_norm(ALPHA * hs_ref[...] + y, g_ref[...], b_ref[...])


def _sample_mixer(hs, ck, cv, st, tabs, w_in, w_ao, w_ro, w_o, g, b, relbt, sinkv, bt):
    nb = hs.shape[0]
    assert nb % bt == 0
    kern = functools.partial(_sample_mixer_kernel, bt)
    out_shape = (
        jax.ShapeDtypeStruct((nb, D_MODEL), F32),
        jax.ShapeDtypeStruct((nb, ATTN_KV, WINDOW), F32),
        jax.ShapeDtypeStruct((nb, ATTN_KV, WINDOW), F32),
        jax.ShapeDtypeStruct((nb, RET_HEADS, RET_DK, RET_DV), F32),
    )
    cache_spec = pl.BlockSpec((bt, ATTN_KV, WINDOW), lambda i: (i, 0, 0))
    return pl.pallas_call(
        kern,
        grid=(nb // bt,),
        in_specs=[
            _const_spec(hs.shape),
            _const_spec(tabs["cos_s"].shape),
            _const_spec(tabs["sin_s"].shape),
            _const_spec(tabs["sbucket"].shape),
            _const_spec(relbt.shape),
            _const_spec(sinkv.shape),
            _smem_spec(),
            _const_spec(w_in.shape),
            _const_spec(w_ao.shape),
            _const_spec(w_ro.shape),
            _const_spec(w_o.shape),
            _const_spec(g.shape),
            _const_spec(b.shape),
            cache_spec,
            cache_spec,
            pl.BlockSpec((bt, RET_HEADS, RET_DK, RET_DV), lambda i: (i, 0, 0, 0)),
        ],
        out_specs=(
            _const_spec((nb, D_MODEL)),
            cache_spec,
            cache_spec,
            pl.BlockSpec((bt, RET_HEADS, RET_DK, RET_DV), lambda i: (i, 0, 0, 0)),
        ),
        out_shape=out_shape,
        scratch_shapes=[
            pltpu.VMEM((nb, IN_COLS), F32),
            pltpu.VMEM((ATTN_HEADS, WINDOW), F32),
            pltpu.VMEM((nb, RET_QK), F32),
            pltpu.VMEM((nb, RET_QK), F32),
            pltpu.VMEM((RET_HEADS, RET_DK, nb), BF16),
            pltpu.VMEM((RET_HEADS, nb, RET_DV), BF16),
            pltpu.VMEM((ATTN_KV, nb), F32),
            pltpu.VMEM((ATTN_KV, nb), F32),
            pltpu.VMEM((nb, RET_V), F32),
            pltpu.VMEM((nb, ATTN_Q), F32),
        ],
        compiler_params=pltpu.CompilerParams(
            dimension_semantics=("arbitrary",), vmem_limit_bytes=VMEM_LIMIT_BYTES),
        name="sample_mixer",
    )(hs, tabs["cos_s"], tabs["sin_s"], tabs["sbucket"], relbt, sinkv, tabs["gamma"],
      w_in, w_ao, w_ro, w_o, g, b, ck, cv, st)


def _rel_bucket_np(dist):
    n = np.maximum(dist, 0)
    max_exact = REL_BUCKETS // 2
    ratio = np.maximum(n, 1).astype(np.float32) / np.float32(max_exact)
    large = max_exact + (np.log(np.maximum(ratio, np.float32(1.0))) / np.float32(math.log(REL_MAX_DIST / max_exact))
                         * np.float32(REL_BUCKETS - max_exact)).astype(np.int32)
    large = np.minimum(large, REL_BUCKETS - 1)
    return np.where(n < max_exact, n, large).astype(np.int32)


def _tables(seq):
    half = RET_DK // 2
    inv = ROPE_BASE ** (-jnp.arange(half, dtype=F32) / half)

    def rope(pos):
        ang = pos.astype(F32)[:, None] * inv[None, :]
        cos, sin = jnp.cos(ang), jnp.sin(ang)
        return jnp.concatenate([cos, cos], axis=-1), jnp.concatenate([-sin, sin], axis=-1)

    cos_p, sin_p = rope(jnp.arange(seq, dtype=jnp.int32))
    cos_s, sin_s = rope(PAST_LEN + jnp.arange(1, dtype=jnp.int32))

    qi = np.arange(BLOCK)[:, None]
    kj = np.arange(2 * BLOCK)[None, :]
    dist = BLOCK + qi - kj
    allowed = (dist >= 0) & (dist < WINDOW)
    bucket = np.where(allowed, _rel_bucket_np(dist), -1).astype(np.int32)
    sdist = WINDOW - np.arange(WINDOW)
    sbucket = np.where(sdist < WINDOW, _rel_bucket_np(sdist), -1).astype(np.int32)
    sbucket = np.broadcast_to(sbucket[None, :], (ATTN_HEADS, WINDOW))

    lg = jnp.log1p(-(2.0 ** (-5.0 - jnp.arange(RET_HEADS, dtype=F32))))
    i = jnp.arange(BLOCK, dtype=F32)
    diff = i[:, None] - i[None, :]
    dmask = jnp.where(diff[None] >= 0, jnp.exp(jnp.maximum(diff, 0.0)[None] * lg[:, None, None]), 0.0)
    ones = jnp.ones((RET_HEADS, BLOCK, RET_DK), F32)
    qdec = jnp.exp((i + 1)[None, :] * lg[:, None])[:, :, None] * ones
    tail = jnp.exp((BLOCK - 1 - i)[None, :] * lg[:, None])[:, :, None] * ones
    return {
        "cos": cos_p, "sin": sin_p, "cos_s": cos_s, "sin_s": sin_s,
        "bucket": jnp.asarray(bucket), "sbucket": jnp.asarray(sbucket),
        "dmask": dmask, "qdec": qdec, "tail": tail,
        "cdec": jnp.exp(BLOCK * lg), "gamma": jnp.exp(lg),
    }


def _ffn_weights(w_up, w_down):
    return w_up.astype(BF16), w_down.astype(BF16)


FFN_TILE = 1024
FFN_SUB = 512
MIX_TILE = 512
SAMPLE_BT = 8


def kernel(x_prompt, x_sample, cache_k_win, cache_v_win, state_ret, rel_bias, w_in, attn_sinks, w_attn_out,
           w_ret_out, w_o, ffn1_w_up, ffn1_w_down, ffn2_w_up, ffn2_w_down, ln1_g, ln1_b, ln2_g, ln2_b, ln3_g, ln3_b):
    bsz, seq, _ = x_prompt.shape
    nb = x_sample.shape[0]
    tabs = _tables(seq)
    hp = x_prompt.reshape(bsz * seq, D_MODEL)
    hs = x_sample.reshape(nb, D_MODEL)
    outs = [[] for _ in range(6)]
    row = lambda v: v.reshape(1, D_MODEL)
    for l in range(DEPTH):
        f1 = _ffn_weights(ffn1_w_up[l], ffn1_w_down[l])
        f2 = _ffn_weights(ffn2_w_up[l], ffn2_w_down[l])
        w_in_b = w_in[l].astype(BF16)
        w_ao_b = w_attn_out[l].astype(BF16)
        w_ro_b = w_ret_out[l].astype(BF16)
        w_o_b = w_o[l].astype(BF16)
        sinks = attn_sinks[l]

        hp = _ffn_ln(hp, *f1, row(ln1_g[l]), row(ln1_b[l]), FFN_TILE, FFN_SUB)
        hs = _ffn_ln(hs, *f1, row(ln1_g[l]), row(ln1_b[l]), nb, nb)

        hp3, kp, vp, sp = _prompt_mixer(hp.reshape(bsz, seq, D_MODEL), tabs, w_in_b, w_ao_b, w_ro_b, w_o_b,
                                        row(ln2_g[l]), row(ln2_b[l]), rel_bias, sinks, MIX_TILE)
        to_t = lambda c: c.transpose(0, 2, 3, 1).reshape(nb, ATTN_KV, WINDOW)
        from_t = lambda c: c.reshape(nb, ATTN_KV_HEADS, ATTN_HEAD_DIM, WINDOW).transpose(0, 3, 1, 2)
        hs, ks, vs, ss = _sample_mixer(
            hs, to_t(cache_k_win[l]), to_t(cache_v_win[l]),
            state_ret[l], tabs, w_in_b, w_ao_b, w_ro_b, w_o_b, row(ln2_g[l]), row(ln2_b[l]),
            rel_bias.T, sinks.reshape(ATTN_HEADS, 1), SAMPLE_BT)
        ks, vs = from_t(ks), from_t(vs)
        hp = hp3.reshape(bsz * seq, D_MODEL)

        hp = _ffn_ln(hp, *f2, row(ln3_g[l]), row(ln3_b[l]), FFN_TILE, FFN_SUB)
        hs = _ffn_ln(hs, *f2, row(ln3_g[l]), row(ln3_b[l]), nb, nb)

        kv_shape = (WINDOW, ATTN_KV_HEADS, ATTN_HEAD_DIM)
        for acc, v in zip(outs, (kp.reshape(bsz, *kv_shape), vp.reshape(bsz, *kv_shape), sp, ks, vs, ss)):
            acc.append(v)
    stacked = [jnp.stack(v) for v in outs]
    return (hp.reshape(bsz, seq, D_MODEL), hs.reshape(nb, 1, D_MODEL), *stacked)
```

```python
import functools
import math

import numpy as np
import jax
import jax.numpy as jnp
from jax import lax
from jax.experimental import pallas as pl
from jax.experimental.pallas import tpu as pltpu

D_MODEL = 1024
DEPTH = 1
PAST_LEN = 16384
ATTN_HEADS = 8
ATTN_KV_HEADS = 2
ATTN_HEAD_DIM = 64
ATTN_GROUP = ATTN_HEADS // ATTN_KV_HEADS
WINDOW = 128
BLOCK = 128
REL_BUCKETS = 32
REL_MAX_DIST = 128
RET_HEADS = 4
RET_DK = 128
RET_DV = 128
ROPE_BASE = 10000.0
D_FF = 2816
LN_EPS = 1e-5
GN_EPS = 1e-6
ALPHA = (2.0 * DEPTH) ** 0.25

ATTN_Q = ATTN_HEADS * ATTN_HEAD_DIM
ATTN_KV = ATTN_KV_HEADS * ATTN_HEAD_DIM
RET_QK = RET_HEADS * RET_DK
RET_V = RET_HEADS * RET_DV
C_QA = 0
C_KA = C_QA + ATTN_Q
C_VA = C_KA + ATTN_KV
C_QR = C_VA + ATTN_KV
C_KR = C_QR + RET_QK
C_VR = C_KR + RET_QK
C_GR = C_VR + RET_V
C_GA = C_GR + RET_V
C_GB = C_GA + D_MODEL
IN_COLS = C_GB + D_MODEL
QKV_GROUPS = ((C_QA, C_QR), (C_QR, C_KR), (C_KR, C_VR), (C_VR, C_GR))
GATE_GROUPS = ((C_GR, C_GA), (C_GA, C_GB), (C_GB, IN_COLS))
PROJ_GROUPS = QKV_GROUPS + GATE_GROUPS

FF_CHUNK = 256
N_FF_CHUNKS = D_FF // FF_CHUNK
NEG = -1e30

F32 = jnp.float32
BF16 = jnp.bfloat16

VMEM_LIMIT_BYTES = 56 * 1024 * 1024


def _dot(a, b):
    return jnp.dot(a, b, preferred_element_type=F32)


def _dot_nt(a, b):
    return lax.dot_general(a, b, (((1,), (1,)), ((), ())), preferred_element_type=F32)


def _dot_tn(a, b):
    return lax.dot_general(a, b, (((0,), (0,)), ((), ())), preferred_element_type=F32)


def _layer_norm(z, g, b):
    mu = jnp.mean(z, axis=-1, keepdims=True)
    zc = z - mu
    var = jnp.mean(zc * zc, axis=-1, keepdims=True)
    return zc * lax.rsqrt(var + LN_EPS) * g + b


def _group_norm(o):
    mu = jnp.mean(o, axis=-1, keepdims=True)
    oc = o - mu
    var = jnp.mean(oc * oc, axis=-1, keepdims=True)
    return oc * lax.rsqrt(var + GN_EPS)


def _silu(x):
    return x * jax.nn.sigmoid(x)


def _const_spec(shape):
    nd = len(shape)
    return pl.BlockSpec(shape, lambda *_: (0,) * nd)


def _resident_spec(shape):
    nd = len(shape)
    return pl.BlockSpec(shape, lambda *_: (0,) * nd, pipeline_mode=pl.Buffered(1))


def _smem_spec():
    return pl.BlockSpec(memory_space=pltpu.SMEM)


def _ffn_ln_kernel(sub, n_side, x_ref, wup_ref, wd_ref, g_ref, b_ref, *refs):
    side_in, o_ref, side_out, act_ref = refs[:n_side], refs[n_side], refs[n_side + 1:-1], refs[-1]
    for src, dst in zip(side_in, side_out):
        dst[...] = src[...].astype(BF16)
    for r0 in range(0, x_ref.shape[0], sub):
        rows = slice(r0, r0 + sub)
        x = x_ref[rows, :]
        xb = x.astype(BF16)
        for c in range(N_FF_CHUNKS):
            c0, c1 = c * FF_CHUNK, (c + 1) * FF_CHUNK
            gate = _dot(xb, wup_ref[:, c0:c1])
            up = _dot(xb, wup_ref[:, D_FF + c0:D_FF + c1])
            act_ref[rows, c0:c1] = (_silu(gate) * up).astype(BF16)
        z = ALPHA * x + 0.5 * _dot(act_ref[rows, :], wd_ref[...])
        o_ref[rows, :] = _layer_norm(z, g_ref[...], b_ref[...])


def _ffn_ln(x2d, wup, wd, g, b, tm, sub, cast_along=()):
    n = x2d.shape[0]
    assert n % tm == 0 and tm % sub == 0
    steps = n // tm
    side_specs = []
    for w in cast_along:
        assert w.ndim == 2 and w.shape[0] % (16 * steps) == 0
        side_specs.append(pl.BlockSpec((w.shape[0] // steps, w.shape[1]), lambda i: (i, 0)))
    row_spec = pl.BlockSpec((tm, D_MODEL), lambda i: (i, 0))
    return pl.pallas_call(
        functools.partial(_ffn_ln_kernel, sub, len(cast_along)),
        grid=(steps,),
        in_specs=[row_spec, _resident_spec(wup.shape), _resident_spec(wd.shape),
                  _const_spec(g.shape), _const_spec(b.shape), *side_specs],
        out_specs=(row_spec, *side_specs),
        out_shape=(jax.ShapeDtypeStruct((n, D_MODEL), F32),
                   *[jax.ShapeDtypeStruct(w.shape, BF16) for w in cast_along]),
        scratch_shapes=[pltpu.VMEM((tm, D_FF), BF16)],
        compiler_params=pltpu.CompilerParams(
            dimension_semantics=("arbitrary",), vmem_limit_bytes=VMEM_LIMIT_BYTES),
        name="ffn_ln",
    )(x2d, wup, wd, g, b, *cast_along)


def _rotary(x, cos_full, sin_signed):
    return x * cos_full + pltpu.roll(x, RET_DK // 2, 1) * sin_signed


def _prompt_mixer_kernel(tile, h_ref, cos_ref, sin_ref, bucket_ref, relb_ref, sinks_ref, dmask_ref,
                         qdec_ref, tail_ref, cdec_ref, w_in_ref, w_ao_ref, w_ro_ref, w_o_ref, g_ref, b_ref,
                         h2_ref, kwin_ref, vwin_ref, sfin_ref,
                         proj_ref, bias_ref, kprev_ref, vprev_ref, s_ref, oattn_ref, r_ref):
    bi = pl.program_id(0)
    ti = pl.program_id(1)

    @pl.when(jnp.logical_and(bi == 0, ti == 0))
    def _build_bias():
        bucket = bucket_ref[...]
        for h in range(ATTN_HEADS):
            def body(k, acc, h=h):
                return jnp.where(bucket == k, relb_ref[k, h], acc)
            bias_ref[h] = lax.fori_loop(0, REL_BUCKETS, body, jnp.full((BLOCK, 2 * BLOCK), NEG, F32))

    @pl.when(ti == 0)
    def _reset_carry():
        kprev_ref[...] = jnp.zeros_like(kprev_ref)
        vprev_ref[...] = jnp.zeros_like(vprev_ref)
        s_ref[...] = jnp.zeros_like(s_ref)

    nsb = tile // BLOCK
    col = lax.broadcasted_iota(jnp.int32, (BLOCK, 2 * BLOCK), 1)

    def rows_of(sb):
        return slice(sb * BLOCK, (sb + 1) * BLOCK)

    def project(sb, groups):
        rows = rows_of(sb)
        hb = h_ref[rows, :].astype(BF16)
        for c0, c1 in groups:
            proj_ref[rows, c0:c1] = _dot(hb, w_in_ref[:, c0:c1])

    def mix_first(sb):
        rows = rows_of(sb)
        ka_cur = proj_ref[rows, C_KA:C_VA]
        va_cur = proj_ref[rows, C_VA:C_QR]
        if sb == 0:
            ka_prev, va_prev = kprev_ref[...], vprev_ref[...]
        else:
            ka_prev = proj_ref[rows_of(sb - 1), C_KA:C_VA]
            va_prev = proj_ref[rows_of(sb - 1), C_VA:C_QR]
        k2 = jnp.concatenate([ka_prev, ka_cur], axis=0)
        v2 = jnp.concatenate([va_prev, va_cur], axis=0)
        probs, vals = [], []
        for kv in range(ATTN_KV_HEADS):
            kk = k2[:, kv * ATTN_HEAD_DIM:(kv + 1) * ATTN_HEAD_DIM].astype(BF16)
            vals.append(v2[:, kv * ATTN_HEAD_DIM:(kv + 1) * ATTN_HEAD_DIM].astype(BF16))
            for g in range(ATTN_GROUP):
                h = kv * ATTN_GROUP + g
                q = proj_ref[rows, h * ATTN_HEAD_DIM:(h + 1) * ATTN_HEAD_DIM].astype(BF16)
                s = _dot_nt(q, kk) * (ATTN_HEAD_DIM ** -0.5) + bias_ref[h]
                if sb == 0:
                    s = jnp.where(jnp.logical_and(ti == 0, col < BLOCK), NEG, s)
                sink = sinks_ref[h]
                m = jnp.maximum(jnp.max(s, axis=-1, keepdims=True), sink)
                p = jnp.exp(s - m)
                denom = jnp.sum(p, axis=-1, keepdims=True) + jnp.exp(sink - m)
                probs.append((p / denom).astype(BF16))
        if sb == nsb - 1:
            kprev_ref[...] = ka_cur
            vprev_ref[...] = va_cur
            kwin_ref[...] = ka_cur
            vwin_ref[...] = va_cur

        cos_b = cos_ref[rows, :]
        sin_b = sin_ref[rows, :]
        scores, vbs, crosses = [], [], []
        for h in range(RET_HEADS):
            qh = _rotary(proj_ref[rows, C_QR + h * RET_DK:C_QR + (h + 1) * RET_DK], cos_b, sin_b)
            kh = _rotary(proj_ref[rows, C_KR + h * RET_DK:C_KR + (h + 1) * RET_DK], cos_b, sin_b) * (RET_DK ** -0.5)
            vb = proj_ref[rows, C_VR + h * RET_DV:C_VR + (h + 1) * RET_DV].astype(BF16)
            scores.append((_dot_nt(qh.astype(BF16), kh.astype(BF16)) * dmask_ref[h]).astype(BF16))
            s_prev = s_ref[h]
            crosses.append(_dot((qh * qdec_ref[h]).astype(BF16), s_prev.astype(BF16)))
            u = _dot_tn((kh * tail_ref[h]).astype(BF16), vb)
            s_ref[h] = cdec_ref[h] * s_prev + u
            vbs.append(vb)
        return probs, vals, scores, vbs, crosses

    def mix_second(sb, operands):
        rows = rows_of(sb)
        probs, vals, scores, vbs, crosses = operands
        for h in range(ATTN_HEADS):
            oattn_ref[rows, h * ATTN_HEAD_DIM:(h + 1) * ATTN_HEAD_DIM] = _dot(probs[h], vals[h // ATTN_GROUP])
        for h in range(RET_HEADS):
            o = _dot(scores[h], vbs[h]) + crosses[h]
            gr = proj_ref[rows, C_GR + h * RET_DV:C_GR + (h + 1) * RET_DV]
            r_ref[rows, h * RET_DV:(h + 1) * RET_DV] = _group_norm(o) * _silu(gr)

    def merge(sb):
        rows = rows_of(sb)
        a_out = _dot(oattn_ref[rows, :].astype(BF16), w_ao_ref[...])
        r_out = _dot(r_ref[rows, :].astype(BF16), w_ro_ref[...])
        merged = (jax.nn.sigmoid(proj_ref[rows, C_GA:C_GB]) * a_out
                  + jax.nn.sigmoid(proj_ref[rows, C_GB:IN_COLS]) * r_out)
        y = _dot(merged.astype(BF16), w_o_ref[...])
        h2_ref[rows, :] = _layer_norm(ALPHA * h_ref[rows, :] + y, g_ref[...], b_ref[...])

    project(0, QKV_GROUPS)
    for sb in range(nsb):
        operands = mix_first(sb)
        project(sb, GATE_GROUPS)
        mix_second(sb, operands)
        if sb + 1 < nsb:
            project(sb + 1, QKV_GROUPS)
        merge(sb)

    @pl.when(ti == pl.num_programs(1) - 1)
    def _emit_state():
        sfin_ref[...] = s_ref[...]


def _prompt_mixer(h, tabs, w_in, w_ao, w_ro, w_o, g, b, relb, sinks, tile):
    bsz, seq, _ = h.shape
    assert seq % tile == 0 and tile % BLOCK == 0
    nt = seq // tile
    kern = functools.partial(_prompt_mixer_kernel, tile)
    out_shape = (
        jax.ShapeDtypeStruct((bsz, seq, D_MODEL), F32),
        jax.ShapeDtypeStruct((bsz, WINDOW, ATTN_KV), F32),
        jax.ShapeDtypeStruct((bsz, WINDOW, ATTN_KV), F32),
        jax.ShapeDtypeStruct((bsz, RET_HEADS, RET_DK, RET_DV), F32),
    )
    return pl.pallas_call(
        kern,
        grid=(bsz, nt),
        in_specs=[
            pl.BlockSpec((None, tile, D_MODEL), lambda i, j: (i, j, 0)),
            pl.BlockSpec((tile, RET_DK), lambda i, j: (j, 0)),
            pl.BlockSpec((tile, RET_DK), lambda i, j: (j, 0)),
            _const_spec(tabs["bucket"].shape),
            _smem_spec(),
            _smem_spec(),
            _const_spec(tabs["dmask"].shape),
            _const_spec(tabs["qdec"].shape),
            _const_spec(tabs["tail"].shape),
            _smem_spec(),
            _const_spec(w_in.shape),
            _const_spec(w_ao.shape),
            _const_spec(w_ro.shape),
            _const_spec(w_o.shape),
            _const_spec(g.shape),
            _const_spec(b.shape),
        ],
        out_specs=(
            pl.BlockSpec((None, tile, D_MODEL), lambda i, j: (i, j, 0)),
            pl.BlockSpec((None, WINDOW, ATTN_KV), lambda i, j: (i, 0, 0)),
            pl.BlockSpec((None, WINDOW, ATTN_KV), lambda i, j: (i, 0, 0)),
            pl.BlockSpec((None, RET_HEADS, RET_DK, RET_DV), lambda i, j: (i, 0, 0, 0)),
        ),
        out_shape=out_shape,
        scratch_shapes=[
            pltpu.VMEM((tile, IN_COLS), F32),
            pltpu.VMEM((ATTN_HEADS, BLOCK, 2 * BLOCK), F32),
            pltpu.VMEM((BLOCK, ATTN_KV), F32),
            pltpu.VMEM((BLOCK, ATTN_KV), F32),
            pltpu.VMEM((RET_HEADS, RET_DK, RET_DV), F32),
            pltpu.VMEM((tile, ATTN_Q), F32),
            pltpu.VMEM((tile, RET_V), F32),
        ],
        compiler_params=pltpu.CompilerParams(
            dimension_semantics=("arbitrary", "arbitrary"), vmem_limit_bytes=VMEM_LIMIT_BYTES),
        name="prompt_mixer",
    )(h, tabs["cos"], tabs["sin"], tabs["bucket"], relb, sinks, tabs["dmask"], tabs["qdec"], tabs["tail"],
      tabs["cdec"], w_in, w_ao, w_ro, w_o, g, b)


def _sample_mixer_kernel(bt, hs_ref, cos_ref, sin_ref, sbucket_ref, relbt_ref, sinkv_ref, gam_ref,
                         w_in_ref, w_ao_ref, w_ro_ref, w_o_ref, g_ref, b_ref, ck_ref, cv_ref, st_ref,
                         ys_ref, kw_ref, vw_ref, stn_ref,
                         proj_ref, sbias_ref, qrot_ref, qdec_ref, kt_ref, vall_ref, knt_ref, vnt_ref,
                         cross_ref, oattn_ref):
    step = pl.program_id(0)
    nb = hs_ref.shape[0]

    @pl.when(step == 0)
    def _project():
        hb = hs_ref[...].astype(BF16)
        for c0, c1 in PROJ_GROUPS:
            proj_ref[:, c0:c1] = _dot(hb, w_in_ref[:, c0:c1])
        bucket = sbucket_ref[...]
        acc = jnp.full((ATTN_HEADS, WINDOW), NEG, F32)
        for k in range(REL_BUCKETS):
            acc = jnp.where(bucket == k, relbt_ref[:, k:k + 1], acc)
        sbias_ref[...] = acc
        cos_b = cos_ref[...]
        sin_b = sin_ref[...]
        for h in range(RET_HEADS):
            hc = slice(h * RET_DK, (h + 1) * RET_DK)
            qh = _rotary(proj_ref[:, C_QR + h * RET_DK:C_QR + (h + 1) * RET_DK], cos_b, sin_b)
            kh = _rotary(proj_ref[:, C_KR + h * RET_DK:C_KR + (h + 1) * RET_DK], cos_b, sin_b) * (RET_DK ** -0.5)
            qrot_ref[:, hc] = qh
            qdec_ref[:, hc] = qh * gam_ref[h]
            proj_ref[:, C_KR + h * RET_DK:C_KR + (h + 1) * RET_DK] = kh
            kt_ref[h] = kh.T.astype(BF16)
            vall_ref[h] = proj_ref[:, C_VR + h * RET_DV:C_VR + (h + 1) * RET_DV].astype(BF16)
        knt_ref[...] = proj_ref[:, C_KA:C_VA].T
        vnt_ref[...] = proj_ref[:, C_VA:C_QR].T

    r0 = pl.multiple_of(step * bt, bt)
    row8 = lax.broadcasted_iota(jnp.int32, (ATTN_HEADS, WINDOW), 0)
    lane8 = lax.broadcasted_iota(jnp.int32, (ATTN_HEADS, WINDOW), 1)
    half = ATTN_HEAD_DIM
    in_kv_half = (row8 < ATTN_GROUP) == (lane8 < half)
    need_roll = (row8 == 1) | (row8 == 3) | (row8 == 4) | (row8 == 6)
    lane_w = lax.broadcasted_iota(jnp.int32, (ATTN_KV, WINDOW), 1)
    rowb = lax.broadcasted_iota(jnp.int32, (bt, RET_DV), 0)
    lane_b = lax.broadcasted_iota(jnp.int32, (RET_DK, nb), 1)
    sinkv = sinkv_ref[...]
    bias_new = relbt_ref[:, 0:1]
    sbias = sbias_ref[...]
    cross_acc = [jnp.zeros((bt, RET_DV), F32) for _ in range(RET_HEADS)]
    oattn_acc = [jnp.zeros((bt, 128), F32) for _ in range(ATTN_Q // 128)]
    qa_tile = [proj_ref[pl.ds(r0, bt), C_QA + k * 128:C_QA + (k + 1) * 128] for k in range(ATTN_Q // 128)]
    ka_tile = proj_ref[pl.ds(r0, bt), C_KA:C_VA]
    va_tile = proj_ref[pl.ds(r0, bt), C_VA:C_QR]

    scale = ATTN_HEAD_DIM ** -0.5
    queries, scores = [], []
    for bl in range(bt):
        qrow = [jnp.broadcast_to(qa_tile[k][bl:bl + 1, :], (ATTN_HEADS, 128)) for k in range(4)]
        gsel = jnp.where(row8 < 2, qrow[0], jnp.where(row8 < 4, qrow[1], jnp.where(row8 < 6, qrow[2], qrow[3])))
        gsel = jnp.where(need_roll, pltpu.roll(gsel, half, 1), gsel)
        qb8 = jnp.where(in_kv_half, gsel, 0.0).astype(BF16)
        queries.append(qb8)
        scores.append(_dot(qb8, ck_ref[bl].astype(BF16)) * scale + sbias)

    for bl in range(bt):
        bg = r0 + bl
        for h in range(RET_HEADS):
            hc = slice(h * RET_DK, (h + 1) * RET_DK)
            s_prev = st_ref[bl, h]
            qd = qdec_ref[pl.ds(r0, bt), hc].astype(BF16)
            res = _dot(qd, s_prev.astype(BF16))
            cross_acc[h] = jnp.where(rowb == bl, res, cross_acc[h])
            k_col = jnp.where(lane_b == bg, kt_ref[h], jnp.zeros_like(kt_ref[h]))
            u = _dot(k_col, vall_ref[h])
            stn_ref[bl, h] = gam_ref[h] * s_prev + u

    for bl in range(bt):
        bg = r0 + bl
        qb8 = queries[bl]
        s = scores[bl]
        knew = ka_tile[bl:bl + 1, :]
        vnew = va_tile[bl:bl + 1, :]
        s_new = jnp.sum(qb8.astype(F32) * knew.astype(BF16).astype(F32), axis=-1, keepdims=True) * scale + bias_new
        m = jnp.maximum(jnp.maximum(jnp.max(s, axis=-1, keepdims=True), s_new), sinkv)
        p = jnp.exp(s - m)
        p_new = jnp.exp(s_new - m)
        denom = jnp.sum(p, axis=-1, keepdims=True) + p_new + jnp.exp(sinkv - m)
        p = p / denom
        p_new = p_new / denom
        vt = cv_ref[bl]
        o8 = _dot_nt(p.astype(BF16), vt.astype(BF16)) + p_new.astype(BF16).astype(F32) * vnew.astype(BF16).astype(F32)
        o8 = jnp.where(need_roll, pltpu.roll(o8, half, 1), o8)
        o8 = jnp.where(((row8 % 2) == 0) == (lane8 < half), o8, 0.0)
        for k in range(4):
            pair = o8[2 * k:2 * k + 1, :] + o8[2 * k + 1:2 * k + 2, :]
            oattn_acc[k] = jnp.where(rowb == bl, pair, oattn_acc[k])
        shift = (WINDOW - 1) - bg
        kw_ref[bl] = jnp.where(lane_w == WINDOW - 1, pltpu.roll(knt_ref[...], shift, 1),
                               pltpu.roll(ck_ref[bl], WINDOW - 1, 1))
        vw_ref[bl] = jnp.where(lane_w == WINDOW - 1, pltpu.roll(vnt_ref[...], shift, 1),
                               pltpu.roll(vt, WINDOW - 1, 1))
    for h in range(RET_HEADS):
        cross_ref[pl.ds(r0, bt), h * RET_DV:(h + 1) * RET_DV] = cross_acc[h]
    for k in range(ATTN_Q // 128):
        oattn_ref[pl.ds(r0, bt), k * 128:(k + 1) * 128] = oattn_acc[k]

    @pl.when(step == pl.num_programs(0) - 1)
    def _merge():
        for h in range(RET_HEADS):
            hc = slice(h * RET_DK, (h + 1) * RET_DK)
            qf = qrot_ref[:, hc].astype(BF16).astype(F32)
            kf = proj_ref[:, C_KR + h * RET_DK:C_KR + (h + 1) * RET_DK].astype(BF16).astype(F32)
            vf = proj_ref[:, C_VR + h * RET_DV:C_VR + (h + 1) * RET_DV].astype(BF16).astype(F32)
            score = jnp.sum(qf * kf, axis=-1, keepdims=True)
            o = score.astype(BF16).astype(F32) * vf + cross_ref[:, hc]
            gr = proj_ref[:, C_GR + h * RET_DV:C_GR + (h + 1) * RET_DV]
            cross_ref[:, hc] = _group_norm(o) * _silu(gr)
        a_out = _dot(oattn_ref[...].astype(BF16), w_ao_ref[...])
        r_out = _dot(cross_ref[...].astype(BF16), w_ro_ref[...])
        merged = (jax.nn.sigmoid(proj_ref[:, C_GA:C_GB]) * a_out
                  + jax.nn.sigmoid(proj_ref[:, C_GB:IN_COLS]) * r_out)
        y = _dot(merged.astype(BF16), w_o_ref[...])
        ys_ref[...] = _layer_norm(ALPHA * hs_ref[...] + y, g_ref[...], b_ref[...])


def _sample_mixer(hs, ck, cv, st, tabs, w_in, w_ao, w_ro, w_o, g, b, relbt, sinkv, bt):
    nb = hs.shape[0]
    assert nb % bt == 0
    kern = functools.partial(_sample_mixer_kernel, bt)
    out_shape = (
        jax.ShapeDtypeStruct((nb, D_MODEL), F32),
        jax.ShapeDtypeStruct((nb, ATTN_KV, WINDOW), F32),
        jax.ShapeDtypeStruct((nb, ATTN_KV, WINDOW), F32),
        jax.ShapeDtypeStruct((nb, RET_HEADS, RET_DK, RET_DV), F32),
    )
    cache_spec = pl.BlockSpec((bt, ATTN_KV, WINDOW), lambda i: (i, 0, 0))
    return pl.pallas_call(
        kern,
        grid=(nb // bt,),
        in_specs=[
            _const_spec(hs.shape),
            _const_spec(tabs["cos_s"].shape),
            _const_spec(tabs["sin_s"].shape),
            _const_spec(tabs["sbucket"].shape),
            _const_spec(relbt.shape),
            _const_spec(sinkv.shape),
            _smem_spec(),
            _const_spec(w_in.shape),
            _const_spec(w_ao.shape),
            _const_spec(w_ro.shape),
            _const_spec(w_o.shape),
            _const_spec(g.shape),
            _const_spec(b.shape),
            cache_spec,
            cache_spec,
            pl.BlockSpec((bt, RET_HEADS, RET_DK, RET_DV), lambda i: (i, 0, 0, 0)),
        ],
        out_specs=(
            _const_spec((nb, D_MODEL)),
            cache_spec,
            cache_spec,
            pl.BlockSpec((bt, RET_HEADS, RET_DK, RET_DV), lambda i: (i, 0, 0, 0)),
        ),
        out_shape=out_shape,
        scratch_shapes=[
            pltpu.VMEM((nb, IN_COLS), F32),
            pltpu.VMEM((ATTN_HEADS, WINDOW), F32),
            pltpu.VMEM((nb, RET_QK), F32),
            pltpu.VMEM((nb, RET_QK), F32),
            pltpu.VMEM((RET_HEADS, RET_DK, nb), BF16),
            pltpu.VMEM((RET_HEADS, nb, RET_DV), BF16),
            pltpu.VMEM((ATTN_KV, nb), F32),
            pltpu.VMEM((ATTN_KV, nb), F32),
            pltpu.VMEM((nb, RET_V), F32),
            pltpu.VMEM((nb, ATTN_Q), F32),
        ],
        compiler_params=pltpu.CompilerParams(
            dimension_semantics=("arbitrary",), vmem_limit_bytes=VMEM_LIMIT_BYTES),
        name="sample_mixer",
    )(hs, tabs["cos_s"], tabs["sin_s"], tabs["sbucket"], relbt, sinkv, tabs["gamma"],
      w_in, w_ao, w_ro, w_o, g, b, ck, cv, st)


def _rel_bucket_np(dist):
    n = np.maximum(dist, 0)
    max_exact = REL_BUCKETS // 2
    ratio = np.maximum(n, 1).astype(np.float32) / np.float32(max_exact)
    large = max_exact + (np.log(np.maximum(ratio, np.float32(1.0))) / np.float32(math.log(REL_MAX_DIST / max_exact))
                         * np.float32(REL_BUCKETS - max_exact)).astype(np.int32)
    large = np.minimum(large, REL_BUCKETS - 1)
    return np.where(n < max_exact, n, large).astype(np.int32)


def _tables(seq):
    half = RET_DK // 2
    inv = ROPE_BASE ** (-jnp.arange(half, dtype=F32) / half)

    def rope(pos):
        ang = pos.astype(F32)[:, None] * inv[None, :]
        cos, sin = jnp.cos(ang), jnp.sin(ang)
        return jnp.concatenate([cos, cos], axis=-1), jnp.concatenate([-sin, sin], axis=-1)

    cos_p, sin_p = rope(jnp.arange(seq, dtype=jnp.int32))
    cos_s, sin_s = rope(PAST_LEN + jnp.arange(1, dtype=jnp.int32))

    qi = np.arange(BLOCK)[:, None]
    kj = np.arange(2 * BLOCK)[None, :]
    dist = BLOCK + qi - kj
    allowed = (dist >= 0) & (dist < WINDOW)
    bucket = np.where(allowed, _rel_bucket_np(dist), -1).astype(np.int32)
    sdist = WINDOW - np.arange(WINDOW)
    sbucket = np.where(sdist < WINDOW, _rel_bucket_np(sdist), -1).astype(np.int32)
    sbucket = np.broadcast_to(sbucket[None, :], (ATTN_HEADS, WINDOW))

    lg = jnp.log1p(-(2.0 ** (-5.0 - jnp.arange(RET_HEADS, dtype=F32))))
    i = jnp.arange(BLOCK, dtype=F32)
    diff = i[:, None] - i[None, :]
    dmask = jnp.where(diff[None] >= 0, jnp.exp(jnp.maximum(diff, 0.0)[None] * lg[:, None, None]), 0.0)
    ones = jnp.ones((RET_HEADS, BLOCK, RET_DK), F32)
    qdec = jnp.exp((i + 1)[None, :] * lg[:, None])[:, :, None] * ones
    tail = jnp.exp((BLOCK - 1 - i)[None, :] * lg[:, None])[:, :, None] * ones
    return {
        "cos": cos_p, "sin": sin_p, "cos_s": cos_s, "sin_s": sin_s,
        "bucket": jnp.asarray(bucket), "sbucket": jnp.asarray(sbucket),
        "dmask": dmask, "qdec": qdec, "tail": tail,
        "cdec": jnp.exp(BLOCK * lg), "gamma": jnp.exp(lg),
    }


def _ffn_weights(w_up, w_down):
    return w_up.astype(BF16), w_down.astype(BF16)


FFN_TILE = 1024
FFN_SUB = 512
MIX_TILE = 512
SAMPLE_BT = 8


def kernel(x_prompt, x_sample, cache_k_win, cache_v_win, state_ret, rel_bias, w_in, attn_sinks, w_attn_out,
           w_ret_out, w_o, ffn1_w_up, ffn1_w_down, ffn2_w_up, ffn2_w_down, ln1_g, ln1_b, ln2_g, ln2_b, ln3_g, ln3_b):
    bsz, seq, _ = x_prompt.shape
    nb = x_sample.shape[0]
    tabs = _tables(seq)
    hp = x_prompt.reshape(bsz * seq, D_MODEL)
    hs = x_sample.reshape(nb, D_MODEL)
    outs = [[] for _ in range(6)]
    row = lambda v: v.reshape(1, D_MODEL)
    for l in range(DEPTH):
        f1 = _ffn_weights(ffn1_w_up[l], ffn1_w_down[l])
        sinks = attn_sinks[l]

        later = (ffn2_w_up[l], ffn2_w_down[l], w_in[l], w_attn_out[l], w_ret_out[l], w_o[l])
        hp, *later_b = _ffn_ln(hp, *f1, row(ln1_g[l]), row(ln1_b[l]), FFN_TILE, FFN_SUB, cast_along=later)
        f2 = later_b[:2]
        w_in_b, w_ao_b, w_ro_b, w_o_b = later_b[2:]
        hs, = _ffn_ln(hs, *f1, row(ln1_g[l]), row(ln1_b[l]), nb, nb)

        hp3, kp, vp, sp = _prompt_mixer(hp.reshape(bsz, seq, D_MODEL), tabs, w_in_b, w_ao_b, w_ro_b, w_o_b,
                                        row(ln2_g[l]), row(ln2_b[l]), rel_bias, sinks, MIX_TILE)
        to_t = lambda c: c.transpose(0, 2, 3, 1).reshape(nb, ATTN_KV, WINDOW)
        from_t = lambda c: c.reshape(nb, ATTN_KV_HEADS, ATTN_HEAD_DIM, WINDOW).transpose(0, 3, 1, 2)
        hs, ks, vs, ss = _sample_mixer(
            hs, to_t(cache_k_win[l]), to_t(cache_v_win[l]),
            state_ret[l], tabs, w_in_b, w_ao_b, w_ro_b, w_o_b, row(ln2_g[l]), row(ln2_b[l]),
            rel_bias.T, sinks.reshape(ATTN_HEADS, 1), SAMPLE_BT)
        ks, vs = from_t(ks), from_t(vs)
        hp = hp3.reshape(bsz * seq, D_MODEL)

        hp, = _ffn_ln(hp, *f2, row(ln3_g[l]), row(ln3_b[l]), FFN_TILE, FFN_SUB)
        hs, = _ffn_ln(hs, *f2, row(ln3_g[l]), row(ln3_b[l]), nb, nb)

        kv_shape = (WINDOW, ATTN_KV_HEADS, ATTN_HEAD_DIM)
        for acc, v in zip(outs, (kp.reshape(bsz, *kv_shape), vp.reshape(bsz, *kv_shape), sp, ks, vs, ss)):
            acc.append(v)
    stacked = [jnp.stack(v) for v in outs]
    return (hp.reshape(bsz, seq, D_MODEL), hs.reshape(nb, 1, D_MODEL), *stacked)
```

```python
import functools
import math

import numpy as np
import jax
import jax.numpy as jnp
from jax import lax
from jax.experimental import pallas as pl
from jax.experimental.pallas import tpu as pltpu

D_MODEL = 1024
DEPTH = 1
PAST_LEN = 16384
ATTN_HEADS = 8
ATTN_KV_HEADS = 2
ATTN_HEAD_DIM = 64
ATTN_GROUP = ATTN_HEADS // ATTN_KV_HEADS
WINDOW = 128
BLOCK = 128
REL_BUCKETS = 32
REL_MAX_DIST = 128
RET_HEADS = 4
RET_DK = 128
RET_DV = 128
ROPE_BASE = 10000.0
D_FF = 2816
LN_EPS = 1e-5
GN_EPS = 1e-6
ALPHA = (2.0 * DEPTH) ** 0.25

ATTN_Q = ATTN_HEADS * ATTN_HEAD_DIM
ATTN_KV = ATTN_KV_HEADS * ATTN_HEAD_DIM
RET_QK = RET_HEADS * RET_DK
RET_V = RET_HEADS * RET_DV
C_QA = 0
C_KA = C_QA + ATTN_Q
C_VA = C_KA + ATTN_KV
C_QR = C_VA + ATTN_KV
C_KR = C_QR + RET_QK
C_VR = C_KR + RET_QK
C_GR = C_VR + RET_V
C_GA = C_GR + RET_V
C_GB = C_GA + D_MODEL
IN_COLS = C_GB + D_MODEL
QKV_GROUPS = ((C_QA, C_QR), (C_QR, C_KR), (C_KR, C_VR), (C_VR, C_GR))
GATE_GROUPS = ((C_GR, C_GA), (C_GA, C_GB), (C_GB, IN_COLS))
PROJ_GROUPS = QKV_GROUPS + GATE_GROUPS

FF_CHUNK = 256
N_FF_CHUNKS = D_FF // FF_CHUNK
NEG = -1e30

F32 = jnp.float32
BF16 = jnp.bfloat16

VMEM_LIMIT_BYTES = 56 * 1024 * 1024


def _dot(a, b):
    return jnp.dot(a, b, preferred_element_type=F32)


def _dot_nt(a, b):
    return lax.dot_general(a, b, (((1,), (1,)), ((), ())), preferred_element_type=F32)


def _dot_tn(a, b):
    return lax.dot_general(a, b, (((0,), (0,)), ((), ())), preferred_element_type=F32)


def _layer_norm(z, g, b):
    mu = jnp.mean(z, axis=-1, keepdims=True)
    zc = z - mu
    var = jnp.mean(zc * zc, axis=-1, keepdims=True)
    return zc * lax.rsqrt(var + LN_EPS) * g + b


def _group_norm(o):
    mu = jnp.mean(o, axis=-1, keepdims=True)
    oc = o - mu
    var = jnp.mean(oc * oc, axis=-1, keepdims=True)
    return oc * lax.rsqrt(var + GN_EPS)


def _silu(x):
    return x * jax.nn.sigmoid(x)


def _const_spec(shape):
    nd = len(shape)
    return pl.BlockSpec(shape, lambda *_: (0,) * nd)


def _resident_spec(shape):
    nd = len(shape)
    return pl.BlockSpec(shape, lambda *_: (0,) * nd, pipeline_mode=pl.Buffered(1))


def _smem_spec():
    return pl.BlockSpec(memory_space=pltpu.SMEM)


def _ffn_ln_kernel(sub, n_side, x_ref, wup_ref, wd_ref, g_ref, b_ref, *refs):
    side_in, o_ref, side_out, act_ref = refs[:n_side], refs[n_side], refs[n_side + 1:-1], refs[-1]
    for src, dst in zip(side_in, side_out):
        dst[...] = src[...].astype(BF16)
    tm = x_ref.shape[0]
    group = min(sub, BLOCK)

    def up_chunk(r0, c):
        rows = slice(r0, r0 + sub)
        c0, c1 = c * FF_CHUNK, (c + 1) * FF_CHUNK
        xb = x_ref[rows, :].astype(BF16)
        gate = _dot(xb, wup_ref[:, c0:c1])
        up = _dot(xb, wup_ref[:, D_FF + c0:D_FF + c1])
        act_ref[rows, c0:c1] = (_silu(gate) * up).astype(BF16)

    def down_group(r0):
        rows = slice(r0, r0 + group)
        z = ALPHA * x_ref[rows, :] + 0.5 * _dot(act_ref[rows, :], wd_ref[...])
        o_ref[rows, :] = _layer_norm(z, g_ref[...], b_ref[...])

    pending = []
    for r0 in range(0, tm, sub):
        for c in range(N_FF_CHUNKS):
            up_chunk(r0, c)
            if pending and c % 2 == 1:
                down_group(pending.pop(0))
        while pending:
            down_group(pending.pop(0))
        pending = list(range(r0, r0 + sub, group))
    while pending:
        down_group(pending.pop(0))


def _ffn_ln(x2d, wup, wd, g, b, tm, sub, cast_along=()):
    n = x2d.shape[0]
    assert n % tm == 0 and tm % sub == 0
    steps = n // tm
    side_specs = []
    for w in cast_along:
        assert w.ndim == 2 and w.shape[0] % (16 * steps) == 0
        side_specs.append(pl.BlockSpec((w.shape[0] // steps, w.shape[1]), lambda i: (i, 0)))
    row_spec = pl.BlockSpec((tm, D_MODEL), lambda i: (i, 0))
    return pl.pallas_call(
        functools.partial(_ffn_ln_kernel, sub, len(cast_along)),
        grid=(steps,),
        in_specs=[row_spec, _resident_spec(wup.shape), _resident_spec(wd.shape),
                  _const_spec(g.shape), _const_spec(b.shape), *side_specs],
        out_specs=(row_spec, *side_specs),
        out_shape=(jax.ShapeDtypeStruct((n, D_MODEL), F32),
                   *[jax.ShapeDtypeStruct(w.shape, BF16) for w in cast_along]),
        scratch_shapes=[pltpu.VMEM((tm, D_FF), BF16)],
        compiler_params=pltpu.CompilerParams(
            dimension_semantics=("arbitrary",), vmem_limit_bytes=VMEM_LIMIT_BYTES),
        name="ffn_ln",
    )(x2d, wup, wd, g, b, *cast_along)


def _rotary(x, cos_full, sin_signed):
    return x * cos_full + pltpu.roll(x, RET_DK // 2, 1) * sin_signed


def _prompt_mixer_kernel(tile, h_ref, cos_ref, sin_ref, bucket_ref, relb_ref, sinks_ref, dmask_ref,
                         qdec_ref, tail_ref, cdec_ref, w_in_ref, w_ao_ref, w_ro_ref, w_o_ref, g_ref, b_ref,
                         h2_ref, kwin_ref, vwin_ref, sfin_ref,
                         proj_ref, bias_ref, kprev_ref, vprev_ref, s_ref, oattn_ref, r_ref):
    bi = pl.program_id(0)
    ti = pl.program_id(1)

    @pl.when(jnp.logical_and(bi == 0, ti == 0))
    def _build_bias():
        bucket = bucket_ref[...]
        for h in range(ATTN_HEADS):
            def body(k, acc, h=h):
                return jnp.where(bucket == k, relb_ref[k, h], acc)
            bias_ref[h] = lax.fori_loop(0, REL_BUCKETS, body, jnp.full((BLOCK, 2 * BLOCK), NEG, F32))

    @pl.when(ti == 0)
    def _reset_carry():
        kprev_ref[...] = jnp.zeros_like(kprev_ref)
        vprev_ref[...] = jnp.zeros_like(vprev_ref)
        s_ref[...] = jnp.zeros_like(s_ref)

    nsb = tile // BLOCK
    col = lax.broadcasted_iota(jnp.int32, (BLOCK, 2 * BLOCK), 1)

    def rows_of(sb):
        return slice(sb * BLOCK, (sb + 1) * BLOCK)

    def project(sb, groups):
        rows = rows_of(sb)
        hb = h_ref[rows, :].astype(BF16)
        for c0, c1 in groups:
            proj_ref[rows, c0:c1] = _dot(hb, w_in_ref[:, c0:c1])

    def mix_first(sb):
        rows = rows_of(sb)
        ka_cur = proj_ref[rows, C_KA:C_VA]
        va_cur = proj_ref[rows, C_VA:C_QR]
        if sb == 0:
            ka_prev, va_prev = kprev_ref[...], vprev_ref[...]
        else:
            ka_prev = proj_ref[rows_of(sb - 1), C_KA:C_VA]
            va_prev = proj_ref[rows_of(sb - 1), C_VA:C_QR]
        k2 = jnp.concatenate([ka_prev, ka_cur], axis=0)
        v2 = jnp.concatenate([va_prev, va_cur], axis=0)
        probs, vals = [], []
        for kv in range(ATTN_KV_HEADS):
            kk = k2[:, kv * ATTN_HEAD_DIM:(kv + 1) * ATTN_HEAD_DIM].astype(BF16)
            vals.append(v2[:, kv * ATTN_HEAD_DIM:(kv + 1) * ATTN_HEAD_DIM].astype(BF16))
            for g in range(ATTN_GROUP):
                h = kv * ATTN_GROUP + g
                q = proj_ref[rows, h * ATTN_HEAD_DIM:(h + 1) * ATTN_HEAD_DIM].astype(BF16)
                s = _dot_nt(q, kk) * (ATTN_HEAD_DIM ** -0.5) + bias_ref[h]
                if sb == 0:
                    s = jnp.where(jnp.logical_and(ti == 0, col < BLOCK), NEG, s)
                sink = sinks_ref[h]
                m = jnp.maximum(jnp.max(s, axis=-1, keepdims=True), sink)
                p = jnp.exp(s - m)
                denom = jnp.sum(p, axis=-1, keepdims=True) + jnp.exp(sink - m)
                probs.append((p / denom).astype(BF16))
        if sb == nsb - 1:
            kprev_ref[...] = ka_cur
            vprev_ref[...] = va_cur
            kwin_ref[...] = ka_cur
            vwin_ref[...] = va_cur

        cos_b = cos_ref[rows, :]
        sin_b = sin_ref[rows, :]
        scores, vbs, crosses = [], [], []
        for h in range(RET_HEADS):
            qh = _rotary(proj_ref[rows, C_QR + h * RET_DK:C_QR + (h + 1) * RET_DK], cos_b, sin_b)
            kh = _rotary(proj_ref[rows, C_KR + h * RET_DK:C_KR + (h + 1) * RET_DK], cos_b, sin_b) * (RET_DK ** -0.5)
            vb = proj_ref[rows, C_VR + h * RET_DV:C_VR + (h + 1) * RET_DV].astype(BF16)
            scores.append((_dot_nt(qh.astype(BF16), kh.astype(BF16)) * dmask_ref[h]).astype(BF16))
            s_prev = s_ref[h]
            crosses.append(_dot((qh * qdec_ref[h]).astype(BF16), s_prev.astype(BF16)))
            u = _dot_tn((kh * tail_ref[h]).astype(BF16), vb)
            s_ref[h] = cdec_ref[h] * s_prev + u
            vbs.append(vb)
        return probs, vals, scores, vbs, crosses

    def mix_second(sb, operands):
        rows = rows_of(sb)
        probs, vals, scores, vbs, crosses = operands
        for h in range(ATTN_HEADS):
            oattn_ref[rows, h * ATTN_HEAD_DIM:(h + 1) * ATTN_HEAD_DIM] = _dot(probs[h], vals[h // ATTN_GROUP])
        for h in range(RET_HEADS):
            o = _dot(scores[h], vbs[h]) + crosses[h]
            gr = proj_ref[rows, C_GR + h * RET_DV:C_GR + (h + 1) * RET_DV]
            r_ref[rows, h * RET_DV:(h + 1) * RET_DV] = _group_norm(o) * _silu(gr)

    def merge(sb):
        rows = rows_of(sb)
        a_out = _dot(oattn_ref[rows, :].astype(BF16), w_ao_ref[...])
        r_out = _dot(r_ref[rows, :].astype(BF16), w_ro_ref[...])
        merged = (jax.nn.sigmoid(proj_ref[rows, C_GA:C_GB]) * a_out
                  + jax.nn.sigmoid(proj_ref[rows, C_GB:IN_COLS]) * r_out)
        y = _dot(merged.astype(BF16), w_o_ref[...])
        h2_ref[rows, :] = _layer_norm(ALPHA * h_ref[rows, :] + y, g_ref[...], b_ref[...])

    project(0, QKV_GROUPS)
    for sb in range(nsb):
        operands = mix_first(sb)
        project(sb, GATE_GROUPS)
        mix_second(sb, operands)
        if sb + 1 < nsb:
            project(sb + 1, QKV_GROUPS)
        merge(sb)

    @pl.when(ti == pl.num_programs(1) - 1)
    def _emit_state():
        sfin_ref[...] = s_ref[...]


def _prompt_mixer(h, tabs, w_in, w_ao, w_ro, w_o, g, b, relb, sinks, tile):
    bsz, seq, _ = h.shape
    assert seq % tile == 0 and tile % BLOCK == 0
    nt = seq // tile
    kern = functools.partial(_prompt_mixer_kernel, tile)
    out_shape = (
        jax.ShapeDtypeStruct((bsz, seq, D_MODEL), F32),
        jax.ShapeDtypeStruct((bsz, WINDOW, ATTN_KV), F32),
        jax.ShapeDtypeStruct((bsz, WINDOW, ATTN_KV), F32),
        jax.ShapeDtypeStruct((bsz, RET_HEADS, RET_DK, RET_DV), F32),
    )
    return pl.pallas_call(
        kern,
        grid=(bsz, nt),
        in_specs=[
            pl.BlockSpec((None, tile, D_MODEL), lambda i, j: (i, j, 0)),
            pl.BlockSpec((tile, RET_DK), lambda i, j: (j, 0)),
            pl.BlockSpec((tile, RET_DK), lambda i, j: (j, 0)),
            _const_spec(tabs["bucket"].shape),
            _smem_spec(),
            _smem_spec(),
            _const_spec(tabs["dmask"].shape),
            _const_spec(tabs["qdec"].shape),
            _const_spec(tabs["tail"].shape),
            _smem_spec(),
            _const_spec(w_in.shape),
            _const_spec(w_ao.shape),
            _const_spec(w_ro.shape),
            _const_spec(w_o.shape),
            _const_spec(g.shape),
            _const_spec(b.shape),
        ],
        out_specs=(
            pl.BlockSpec((None, tile, D_MODEL), lambda i, j: (i, j, 0)),
            pl.BlockSpec((None, WINDOW, ATTN_KV), lambda i, j: (i, 0, 0)),
            pl.BlockSpec((None, WINDOW, ATTN_KV), lambda i, j: (i, 0, 0)),
            pl.BlockSpec((None, RET_HEADS, RET_DK, RET_DV), lambda i, j: (i, 0, 0, 0)),
        ),
        out_shape=out_shape,
        scratch_shapes=[
            pltpu.VMEM((tile, IN_COLS), F32),
            pltpu.VMEM((ATTN_HEADS, BLOCK, 2 * BLOCK), F32),
            pltpu.VMEM((BLOCK, ATTN_KV), F32),
            pltpu.VMEM((BLOCK, ATTN_KV), F32),
            pltpu.VMEM((RET_HEADS, RET_DK, RET_DV), F32),
            pltpu.VMEM((tile, ATTN_Q), F32),
            pltpu.VMEM((tile, RET_V), F32),
        ],
        compiler_params=pltpu.CompilerParams(
            dimension_semantics=("arbitrary", "arbitrary"), vmem_limit_bytes=VMEM_LIMIT_BYTES),
        name="prompt_mixer",
    )(h, tabs["cos"], tabs["sin"], tabs["bucket"], relb, sinks, tabs["dmask"], tabs["qdec"], tabs["tail"],
      tabs["cdec"], w_in, w_ao, w_ro, w_o, g, b)


def _sample_mixer_kernel(bt, hs_ref, cos_ref, sin_ref, sbucket_ref, relbt_ref, sinkv_ref, gam_ref,
                         w_in_ref, w_ao_ref, w_ro_ref, w_o_ref, g_ref, b_ref, ck_ref, cv_ref, st_ref,
                         ys_ref, kw_ref, vw_ref, stn_ref,
                         proj_ref, sbias_ref, qrot_ref, qdec_ref, kt_ref, vall_ref, knt_ref, vnt_ref,
                         cross_ref, oattn_ref):
    step = pl.program_id(0)
    nb = hs_ref.shape[0]

    @pl.when(step == 0)
    def _project():
        hb = hs_ref[...].astype(BF16)
        for c0, c1 in PROJ_GROUPS:
            proj_ref[:, c0:c1] = _dot(hb, w_in_ref[:, c0:c1])
        bucket = sbucket_ref[...]
        acc = jnp.full((ATTN_HEADS, WINDOW), NEG, F32)
        for k in range(REL_BUCKETS):
            acc = jnp.where(bucket == k, relbt_ref[:, k:k + 1], acc)
        sbias_ref[...] = acc
        cos_b = cos_ref[...]
        sin_b = sin_ref[...]
        for h in range(RET_HEADS):
            hc = slice(h * RET_DK, (h + 1) * RET_DK)
            qh = _rotary(proj_ref[:, C_QR + h * RET_DK:C_QR + (h + 1) * RET_DK], cos_b, sin_b)
            kh = _rotary(proj_ref[:, C_KR + h * RET_DK:C_KR + (h + 1) * RET_DK], cos_b, sin_b) * (RET_DK ** -0.5)
            qrot_ref[:, hc] = qh
            qdec_ref[:, hc] = qh * gam_ref[h]
            proj_ref[:, C_KR + h * RET_DK:C_KR + (h + 1) * RET_DK] = kh
            kt_ref[h] = kh.T.astype(BF16)
            vall_ref[h] = proj_ref[:, C_VR + h * RET_DV:C_VR + (h + 1) * RET_DV].astype(BF16)
        knt_ref[...] = proj_ref[:, C_KA:C_VA].T
        vnt_ref[...] = proj_ref[:, C_VA:C_QR].T

    r0 = pl.multiple_of(step * bt, bt)
    row8 = lax.broadcasted_iota(jnp.int32, (ATTN_HEADS, WINDOW), 0)
    lane8 = lax.broadcasted_iota(jnp.int32, (ATTN_HEADS, WINDOW), 1)
    half = ATTN_HEAD_DIM
    in_kv_half = (row8 < ATTN_GROUP) == (lane8 < half)
    need_roll = (row8 == 1) | (row8 == 3) | (row8 == 4) | (row8 == 6)
    lane_w = lax.broadcasted_iota(jnp.int32, (ATTN_KV, WINDOW), 1)
    rowb = lax.broadcasted_iota(jnp.int32, (bt, RET_DV), 0)
    lane_b = lax.broadcasted_iota(jnp.int32, (RET_DK, nb), 1)
    sinkv = sinkv_ref[...]
    bias_new = relbt_ref[:, 0:1]
    sbias = sbias_ref[...]
    cross_acc = [jnp.zeros((bt, RET_DV), F32) for _ in range(RET_HEADS)]
    oattn_acc = [jnp.zeros((bt, 128), F32) for _ in range(ATTN_Q // 128)]
    qa_tile = [proj_ref[pl.ds(r0, bt), C_QA + k * 128:C_QA + (k + 1) * 128] for k in range(ATTN_Q // 128)]
    ka_tile = proj_ref[pl.ds(r0, bt), C_KA:C_VA]
    va_tile = proj_ref[pl.ds(r0, bt), C_VA:C_QR]

    scale = ATTN_HEAD_DIM ** -0.5
    queries, scores = [], []
    for bl in range(bt):
        qrow = [jnp.broadcast_to(qa_tile[k][bl:bl + 1, :], (ATTN_HEADS, 128)) for k in range(4)]
        gsel = jnp.where(row8 < 2, qrow[0], jnp.where(row8 < 4, qrow[1], jnp.where(row8 < 6, qrow[2], qrow[3])))
        gsel = jnp.where(need_roll, pltpu.roll(gsel, half, 1), gsel)
        qb8 = jnp.where(in_kv_half, gsel, 0.0).astype(BF16)
        queries.append(qb8)
        scores.append(_dot(qb8, ck_ref[bl].astype(BF16)) * scale + sbias)

    for bl in range(bt):
        bg = r0 + bl
        for h in range(RET_HEADS):
            hc = slice(h * RET_DK, (h + 1) * RET_DK)
            s_prev = st_ref[bl, h]
            qd = qdec_ref[pl.ds(r0, bt), hc].astype(BF16)
            res = _dot(qd, s_prev.astype(BF16))
            cross_acc[h] = jnp.where(rowb == bl, res, cross_acc[h])
            k_col = jnp.where(lane_b == bg, kt_ref[h], jnp.zeros_like(kt_ref[h]))
            u = _dot(k_col, vall_ref[h])
            stn_ref[bl, h] = gam_ref[h] * s_prev + u

    for bl in range(bt):
        bg = r0 + bl
        qb8 = queries[bl]
        s = scores[bl]
        knew = ka_tile[bl:bl + 1, :]
        vnew = va_tile[bl:bl + 1, :]
        s_new = jnp.sum(qb8.astype(F32) * knew.astype(BF16).astype(F32), axis=-1, keepdims=True) * scale + bias_new
        m = jnp.maximum(jnp.maximum(jnp.max(s, axis=-1, keepdims=True), s_new), sinkv)
        p = jnp.exp(s - m)
        p_new = jnp.exp(s_new - m)
        denom = jnp.sum(p, axis=-1, keepdims=True) + p_new + jnp.exp(sinkv - m)
        p = p / denom
        p_new = p_new / denom
        vt = cv_ref[bl]
        o8 = _dot_nt(p.astype(BF16), vt.astype(BF16)) + p_new.astype(BF16).astype(F32) * vnew.astype(BF16).astype(F32)
        o8 = jnp.where(need_roll, pltpu.roll(o8, half, 1), o8)
        o8 = jnp.where(((row8 % 2) == 0) == (lane8 < half), o8, 0.0)
        for k in range(4):
            pair = o8[2 * k:2 * k + 1, :] + o8[2 * k + 1:2 * k + 2, :]
            oattn_acc[k] = jnp.where(rowb == bl, pair, oattn_acc[k])
        shift = (WINDOW - 1) - bg
        kw_ref[bl] = jnp.where(lane_w == WINDOW - 1, pltpu.roll(knt_ref[...], shift, 1),
                               pltpu.roll(ck_ref[bl], WINDOW - 1, 1))
        vw_ref[bl] = jnp.where(lane_w == WINDOW - 1, pltpu.roll(vnt_ref[...], shift, 1),
                               pltpu.roll(vt, WINDOW - 1, 1))
    for h in range(RET_HEADS):
        cross_ref[pl.ds(r0, bt), h * RET_DV:(h + 1) * RET_DV] = cross_acc[h]
    for k in range(ATTN_Q // 128):
        oattn_ref[pl.ds(r0, bt), k * 128:(k + 1) * 128] = oattn_acc[k]

    @pl.when(step == pl.num_programs(0) - 1)
    def _merge():
        for h in range(RET_HEADS):
            hc = slice(h * RET_DK, (h + 1) * RET_DK)
            qf = qrot_ref[:, hc].astype(BF16).astype(F32)
            kf = proj_ref[:, C_KR + h * RET_DK:C_KR + (h + 1) * RET_DK].astype(BF16).astype(F32)
            vf = proj_ref[:, C_VR + h * RET_DV:C_VR + (h + 1) * RET_DV].astype(BF16).astype(F32)
            score = jnp.sum(qf * kf, axis=-1, keepdims=True)
            o = score.astype(BF16).astype(F32) * vf + cross_ref[:, hc]
            gr = proj_ref[:, C_GR + h * RET_DV:C_GR + (h + 1) * RET_DV]
            cross_ref[:, hc] = _group_norm(o) * _silu(gr)
        a_out = _dot(oattn_ref[...].astype(BF16), w_ao_ref[...])
        r_out = _dot(cross_ref[...].astype(BF16), w_ro_ref[...])
        merged = (jax.nn.sigmoid(proj_ref[:, C_GA:C_GB]) * a_out
                  + jax.nn.sigmoid(proj_ref[:, C_GB:IN_COLS]) * r_out)
        y = _dot(merged.astype(BF16), w_o_ref[...])
        ys_ref[...] = _layer_norm(ALPHA * hs_ref[...] + y, g_ref[...], b_ref[...])


def _sample_mixer(hs, ck, cv, st, tabs, w_in, w_ao, w_ro, w_o, g, b, relbt, sinkv, bt):
    nb = hs.shape[0]
    assert nb % bt == 0
    kern = functools.partial(_sample_mixer_kernel, bt)
    out_shape = (
        jax.ShapeDtypeStruct((nb, D_MODEL), F32),
        jax.ShapeDtypeStruct((nb, ATTN_KV, WINDOW), F32),
        jax.ShapeDtypeStruct((nb, ATTN_KV, WINDOW), F32),
        jax.ShapeDtypeStruct((nb, RET_HEADS, RET_DK, RET_DV), F32),
    )
    cache_spec = pl.BlockSpec((bt, ATTN_KV, WINDOW), lambda i: (i, 0, 0))
    return pl.pallas_call(
        kern,
        grid=(nb // bt,),
        in_specs=[
            _const_spec(hs.shape),
            _const_spec(tabs["cos_s"].shape),
            _const_spec(tabs["sin_s"].shape),
            _const_spec(tabs["sbucket"].shape),
            _const_spec(relbt.shape),
            _const_spec(sinkv.shape),
            _smem_spec(),
            _const_spec(w_in.shape),
            _const_spec(w_ao.shape),
            _const_spec(w_ro.shape),
            _const_spec(w_o.shape),
            _const_spec(g.shape),
            _const_spec(b.shape),
            cache_spec,
            cache_spec,
            pl.BlockSpec((bt, RET_HEADS, RET_DK, RET_DV), lambda i: (i, 0, 0, 0)),
        ],
        out_specs=(
            _const_spec((nb, D_MODEL)),
            cache_spec,
            cache_spec,
            pl.BlockSpec((bt, RET_HEADS, RET_DK, RET_DV), lambda i: (i, 0, 0, 0)),
        ),
        out_shape=out_shape,
        scratch_shapes=[
            pltpu.VMEM((nb, IN_COLS), F32),
            pltpu.VMEM((ATTN_HEADS, WINDOW), F32),
            pltpu.VMEM((nb, RET_QK), F32),
            pltpu.VMEM((nb, RET_QK), F32),
            pltpu.VMEM((RET_HEADS, RET_DK, nb), BF16),
            pltpu.VMEM((RET_HEADS, nb, RET_DV), BF16),
            pltpu.VMEM((ATTN_KV, nb), F32),
            pltpu.VMEM((ATTN_KV, nb), F32),
            pltpu.VMEM((nb, RET_V), F32),
            pltpu.VMEM((nb, ATTN_Q), F32),
        ],
        compiler_params=pltpu.CompilerParams(
            dimension_semantics=("arbitrary",), vmem_limit_bytes=VMEM_LIMIT_BYTES),
        name="sample_mixer",
    )(hs, tabs["cos_s"], tabs["sin_s"], tabs["sbucket"], relbt, sinkv, tabs["gamma"],
      w_in, w_ao, w_ro, w_o, g, b, ck, cv, st)


def _rel_bucket_np(dist):
    n = np.maximum(dist, 0)
    max_exact = REL_BUCKETS // 2
    ratio = np.maximum(n, 1).astype(np.float32) / np.float32(max_exact)
    large = max_exact + (np.log(np.maximum(ratio, np.float32(1.0))) / np.float32(math.log(REL_MAX_DIST / max_exact))
                         * np.float32(REL_BUCKETS - max_exact)).astype(np.int32)
    large = np.minimum(large, REL_BUCKETS - 1)
    return np.where(n < max_exact, n, large).astype(np.int32)


def _tables(seq):
    half = RET_DK // 2
    inv = ROPE_BASE ** (-jnp.arange(half, dtype=F32) / half)

    def rope(pos):
        ang = pos.astype(F32)[:, None] * inv[None, :]
        cos, sin = jnp.cos(ang), jnp.sin(ang)
        return jnp.concatenate([cos, cos], axis=-1), jnp.concatenate([-sin, sin], axis=-1)

    cos_p, sin_p = rope(jnp.arange(seq, dtype=jnp.int32))
    cos_s, sin_s = rope(PAST_LEN + jnp.arange(1, dtype=jnp.int32))

    qi = np.arange(BLOCK)[:, None]
    kj = np.arange(2 * BLOCK)[None, :]
    dist = BLOCK + qi - kj
    allowed = (dist >= 0) & (dist < WINDOW)
    bucket = np.where(allowed, _rel_bucket_np(dist), -1).astype(np.int32)
    sdist = WINDOW - np.arange(WINDOW)
    sbucket = np.where(sdist < WINDOW, _rel_bucket_np(sdist), -1).astype(np.int32)
    sbucket = np.broadcast_to(sbucket[None, :], (ATTN_HEADS, WINDOW))

    lg = jnp.log1p(-(2.0 ** (-5.0 - jnp.arange(RET_HEADS, dtype=F32))))
    i = jnp.arange(BLOCK, dtype=F32)
    diff = i[:, None] - i[None, :]
    dmask = jnp.where(diff[None] >= 0, jnp.exp(jnp.maximum(diff, 0.0)[None] * lg[:, None, None]), 0.0)
    ones = jnp.ones((RET_HEADS, BLOCK, RET_DK), F32)
    qdec = jnp.exp((i + 1)[None, :] * lg[:, None])[:, :, None] * ones
    tail = jnp.exp((BLOCK - 1 - i)[None, :] * lg[:, None])[:, :, None] * ones
    return {
        "cos": cos_p, "sin": sin_p, "cos_s": cos_s, "sin_s": sin_s,
        "bucket": jnp.asarray(bucket), "sbucket": jnp.asarray(sbucket),
        "dmask": dmask, "qdec": qdec, "tail": tail,
        "cdec": jnp.exp(BLOCK * lg), "gamma": jnp.exp(lg),
    }


def _ffn_weights(w_up, w_down):
    return w_up.astype(BF16), w_down.astype(BF16)


FFN_TILE = 1024
FFN_SUB = 512
MIX_TILE = 512
SAMPLE_BT = 8


def kernel(x_prompt, x_sample, cache_k_win, cache_v_win, state_ret, rel_bias, w_in, attn_sinks, w_attn_out,
           w_ret_out, w_o, ffn1_w_up, ffn1_w_down, ffn2_w_up, ffn2_w_down, ln1_g, ln1_b, ln2_g, ln2_b, ln3_g, ln3_b):
    bsz, seq, _ = x_prompt.shape
    nb = x_sample.shape[0]
    tabs = _tables(seq)
    hp = x_prompt.reshape(bsz * seq, D_MODEL)
    hs = x_sample.reshape(nb, D_MODEL)
    outs = [[] for _ in range(6)]
    row = lambda v: v.reshape(1, D_MODEL)
    for l in range(DEPTH):
        f1 = _ffn_weights(ffn1_w_up[l], ffn1_w_down[l])
        sinks = attn_sinks[l]

        later = (ffn2_w_up[l], ffn2_w_down[l], w_in[l], w_attn_out[l], w_ret_out[l], w_o[l])
        hp, *later_b = _ffn_ln(hp, *f1, row(ln1_g[l]), row(ln1_b[l]), FFN_TILE, FFN_SUB, cast_along=later)
        f2 = later_b[:2]
        w_in_b, w_ao_b, w_ro_b, w_o_b = later_b[2:]
        hs, = _ffn_ln(hs, *f1, row(ln1_g[l]), row(ln1_b[l]), nb, nb)

        hp3, kp, vp, sp = _prompt_mixer(hp.reshape(bsz, seq, D_MODEL), tabs, w_in_b, w_ao_b, w_ro_b, w_o_b,
                                        row(ln2_g[l]), row(ln2_b[l]), rel_bias, sinks, MIX_TILE)
        to_t = lambda c: c.transpose(0, 2, 3, 1).reshape(nb, ATTN_KV, WINDOW)
        from_t = lambda c: c.reshape(nb, ATTN_KV_HEADS, ATTN_HEAD_DIM, WINDOW).transpose(0, 3, 1, 2)
        hs, ks, vs, ss = _sample_mixer(
            hs, to_t(cache_k_win[l]), to_t(cache_v_win[l]),
            state_ret[l], tabs, w_in_b, w_ao_b, w_ro_b, w_o_b, row(ln2_g[l]), row(ln2_b[l]),
            rel_bias.T, sinks.reshape(ATTN_HEADS, 1), SAMPLE_BT)
        ks, vs = from_t(ks), from_t(vs)
        hp = hp3.reshape(bsz * seq, D_MODEL)

        hp, = _ffn_ln(hp, *f2, row(ln3_g[l]), row(ln3_b[l]), FFN_TILE, FFN_SUB)
        hs, = _ffn_ln(hs, *f2, row(ln3_g[l]), row(ln3_b[l]), nb, nb)

        kv_shape = (WINDOW, ATTN_KV_HEADS, ATTN_HEAD_DIM)
        for acc, v in zip(outs, (kp.reshape(bsz, *kv_shape), vp.reshape(bsz, *kv_shape), sp, ks, vs, ss)):
            acc.append(v)
    stacked = [jnp.stack(v) for v in outs]
    return (hp.reshape(bsz, seq, D_MODEL), hs.reshape(nb, 1, D_MODEL), *stacked)
```
